```python
import jax, jax.numpy as jnp
from jax import lax
import numpy as np

D_MODEL = 1024
BATCH = 1
SEQ = 16384
DEPTH = 1
DEC_BATCH = 8
DEC_SEQ = 4096
PAST_LEN = 128

N_MEM = 256
BLK = 128
WINDOW = 128
ROPE_THETA = 10000.0
EPS = 1e-6
NEG = -1e30
A_HEADS = 8
A_KV_HEADS = 2
A_HEAD_DIM = 64
A_GROUP = A_HEADS // A_KV_HEADS
A_Q = A_HEADS * A_HEAD_DIM
A_KV = A_KV_HEADS * A_HEAD_DIM
B_HEADS = 8
B_Q_LORA = 384
B_KV_LORA = 256
B_NOPE = 64
B_ROPE = 32
B_V = 64
B_OUT = B_HEADS * B_V
C_HEADS = 4
C_HEAD_DIM = 128
C_Q = C_HEADS * C_HEAD_DIM
N_BRANCH = 3
D_FF = 2816
IN_SIZES = (A_Q, A_KV, A_KV, B_Q_LORA, B_KV_LORA, B_ROPE, C_Q, N_BRANCH * D_MODEL)
IN_COLS = A_Q + 2 * A_KV + B_Q_LORA + B_KV_LORA + B_ROPE + C_Q + N_BRANCH * D_MODEL

kernel_name = "hybrid_bidir_encoder_gated_branches"


def rms_norm(x, g):
    xf = x.astype(jnp.float32)
    y = xf * lax.rsqrt(jnp.mean(xf * xf, axis=-1, keepdims=True) + EPS)
    return (y * g.astype(jnp.float32)).astype(x.dtype)


def rope(x):
    S, d = x.shape[1], x.shape[-1]
    inv = ROPE_THETA ** (-jnp.arange(0, d, 2, dtype=jnp.float32) / d)
    ang = jnp.arange(S, dtype=jnp.float32)[:, None] * inv[None, :]
    shape = (S,) + (1,) * (x.ndim - 3) + (d // 2,)
    cos = jnp.cos(ang).reshape(shape)
    sin = jnp.sin(ang).reshape(shape)
    xf = x.astype(jnp.float32)
    x1, x2 = xf[..., : d // 2], xf[..., d // 2:]
    return jnp.concatenate([x1 * cos - x2 * sin, x2 * cos + x1 * sin], axis=-1).astype(x.dtype)


def swiglu(x, w_in, w_out):
    gu = x @ w_in
    g, u = gu[..., :D_FF], gu[..., D_FF:]
    return (jax.nn.silu(g) * u) @ w_out


def split_cols(z, sizes):
    out, o = [], 0
    for s in sizes:
        out.append(z[..., o:o + s])
        o += s
    return out


def window_gqa(q, k, v, sink):
    B, S = q.shape[0], q.shape[1]
    nb = S // BLK
    qb = q.reshape(B, nb, BLK, A_KV_HEADS, A_GROUP, A_HEAD_DIM)

    def band(t):
        tp = jnp.pad(t, ((0, 0), (BLK, BLK), (0, 0), (0, 0))).reshape(B, nb + 2, BLK, A_KV_HEADS, A_HEAD_DIM)
        return jnp.concatenate([tp[:, :-2], tp[:, 1:-1], tp[:, 2:]], axis=2)

    kb, vb = band(k), band(v)
    s = jnp.einsum('bnqkgd,bnjkd->bnkgqj', qb, kb).astype(jnp.float32) * (A_HEAD_DIM ** -0.5)
    qpos = jnp.arange(nb)[:, None, None] * BLK + jnp.arange(BLK)[None, :, None]
    kpos = jnp.arange(nb)[:, None, None] * BLK + jnp.arange(3 * BLK)[None, None, :] - BLK
    valid = (jnp.abs(kpos - qpos) <= WINDOW) & (kpos >= 0) & (kpos < S)
    s = jnp.where(valid[None, :, None, None], s, NEG)
    sink_b = sink.astype(jnp.float32).reshape(1, 1, A_KV_HEADS, A_GROUP, 1, 1)
    m = jnp.maximum(jnp.max(s, axis=-1, keepdims=True), sink_b)
    e = jnp.exp(s - m)
    p = e / (jnp.sum(e, axis=-1, keepdims=True) + jnp.exp(sink_b - m))
    o = jnp.einsum('bnkgqj,bnjkd->bnqkgd', p.astype(v.dtype), vb)
    return o.reshape(B, S, A_Q)


def mla(c_q, c_kv, k_rope_raw, q_norm, w_uq, kv_norm, w_ukv):
    B, S = c_q.shape[0], c_q.shape[1]
    nb = S // BLK
    q = (rms_norm(c_q, q_norm) @ w_uq).reshape(B, S, B_HEADS, B_NOPE + B_ROPE)
    q_nope, q_rope = q[..., :B_NOPE], rope(q[..., B_NOPE:])
    kv = (rms_norm(c_kv, kv_norm) @ w_ukv).reshape(B, S, B_HEADS, B_NOPE + B_V)
    k_nope, v = kv[..., :B_NOPE], kv[..., B_NOPE:]
    k_rope = rope(k_rope_raw)
    qn = q_nope.reshape(B, nb, BLK, B_HEADS, B_NOPE).swapaxes(0, 1)
    qr = q_rope.reshape(B, nb, BLK, B_HEADS, B_ROPE).swapaxes(0, 1)
    scale = (B_NOPE + B_ROPE) ** -0.5

    def one_block(args):
        qn_b, qr_b = args
        s = (jnp.einsum('bqhd,bkhd->bhqk', qn_b, k_nope).astype(jnp.float32)
             + jnp.einsum('bqhr,bkr->bhqk', qr_b, k_rope).astype(jnp.float32)) * scale
        p = jax.nn.softmax(s, axis=-1).astype(v.dtype)
        return jnp.einsum('bhqk,bkhd->bqhd', p, v)

    o = lax.map(one_block, (qn, qr))
    return o.swapaxes(0, 1).reshape(B, S, B_OUT)


def mem_attn(q, mem, mem_norm, w_mem_kv):
    B, S = q.shape[0], q.shape[1]
    kv = rms_norm(mem, mem_norm) @ w_mem_kv
    k = kv[..., :C_Q].reshape(B, N_MEM, C_HEADS, C_HEAD_DIM)
    v = kv[..., C_Q:].reshape(B, N_MEM, C_HEADS, C_HEAD_DIM)
    qh = q.reshape(B, S, C_HEADS, C_HEAD_DIM)
    s = jnp.einsum('bqhd,bmhd->bhqm', qh, k).astype(jnp.float32) * (C_HEAD_DIM ** -0.5)
    p = jax.nn.softmax(s, axis=-1).astype(v.dtype)
    return jnp.einsum('bhqm,bmhd->bqhd', p, v).reshape(B, S, C_Q)


def encoder_layer(x, mem, ffn1_norm, ffn1_w_in, ffn1_w_out, mix_norm, w_in,
                  mla_q_norm, mla_w_uq, mla_kv_norm, mla_w_ukv, attn_sink,
                  mem_norm, w_mem_kv, w_branch_a, w_branch_b, w_branch_c, w_out,
                  ffn2_norm, ffn2_w_in, ffn2_w_out):
    B, S = x.shape[0], x.shape[1]
    x = x + 0.5 * swiglu(rms_norm(x, ffn1_norm), ffn1_w_in, ffn1_w_out)
    u = rms_norm(x, mix_norm)
    z = u @ w_in
    qa, ka, va, cq, ckv, kr, qc, gl = split_cols(z, IN_SIZES)
    qa = rope(qa.reshape(B, S, A_HEADS, A_HEAD_DIM))
    ka = rope(ka.reshape(B, S, A_KV_HEADS, A_HEAD_DIM))
    va = va.reshape(B, S, A_KV_HEADS, A_HEAD_DIM)
    oa = window_gqa(qa, ka, va, attn_sink) @ w_branch_a
    ob = mla(cq, ckv, kr, mla_q_norm, mla_w_uq, mla_kv_norm, mla_w_ukv) @ w_branch_b
    oc = mem_attn(qc, mem, mem_norm, w_mem_kv) @ w_branch_c
    gates = jax.nn.sigmoid(gl.reshape(B, S, N_BRANCH, D_MODEL))
    merged = gates[:, :, 0] * oa + gates[:, :, 1] * ob + gates[:, :, 2] * oc
    x = x + merged @ w_out
    x = x + 0.5 * swiglu(rms_norm(x, ffn2_norm), ffn2_w_in, ffn2_w_out)
    return x


def trunk(x, mem, layer_params, final_norm):
    for l in range(DEPTH):
        x = encoder_layer(x, mem, *[p[l] for p in layer_params])
    return rms_norm(x, final_norm)


def setup_inputs(seed: int = 0) -> dict:
    key = jax.random.key(seed)
    ks = jax.random.split(key, 24)

    def dense(k, shape, fan_in):
        return jax.random.normal(k, shape, jnp.float32) * (fan_in ** -0.5)

    def gain(k, shape):
        return 1.0 + 0.02 * jax.random.normal(k, shape, jnp.float32)

    L = DEPTH
    return {
        "x_prompt": jax.random.normal(ks[0], (BATCH, SEQ, D_MODEL), jnp.float32),
        "x_sample": jax.random.normal(ks[1], (DEC_BATCH, DEC_SEQ, D_MODEL), jnp.float32),
        "mem_prompt": jax.random.normal(ks[2], (BATCH, N_MEM, D_MODEL), jnp.float32),
        "mem_sample": jax.random.normal(ks[3], (DEC_BATCH, N_MEM, D_MODEL), jnp.float32),
        "ffn1_norm": gain(ks[4], (L, D_MODEL)),
        "ffn1_w_in": dense(ks[5], (L, D_MODEL, 2 * D_FF), D_MODEL),
        "ffn1_w_out": dense(ks[6], (L, D_FF, D_MODEL), D_FF),
        "mix_norm": gain(ks[7], (L, D_MODEL)),
        "w_in": dense(ks[8], (L, D_MODEL, IN_COLS), D_MODEL),
        "mla_q_norm": gain(ks[9], (L, B_Q_LORA)),
        "mla_w_uq": dense(ks[10], (L, B_Q_LORA, B_HEADS * (B_NOPE + B_ROPE)), B_Q_LORA),
        "mla_kv_norm": gain(ks[11], (L, B_KV_LORA)),
        "mla_w_ukv": dense(ks[12], (L, B_KV_LORA, B_HEADS * (B_NOPE + B_V)), B_KV_LORA),
        "attn_sink": 0.5 * jax.random.normal(ks[13], (L, A_HEADS), jnp.float32),
        "mem_norm": gain(ks[14], (L, D_MODEL)),
        "w_mem_kv": dense(ks[15], (L, D_MODEL, 2 * C_Q), D_MODEL),
        "w_branch_a": dense(ks[16], (L, A_Q, D_MODEL), A_Q),
        "w_branch_b": dense(ks[17], (L, B_OUT, D_MODEL), B_OUT),
        "w_branch_c": dense(ks[18], (L, C_Q, D_MODEL), C_Q),
        "w_out": dense(ks[19], (L, D_MODEL, D_MODEL), D_MODEL),
        "ffn2_norm": gain(ks[20], (L, D_MODEL)),
        "ffn2_w_in": dense(ks[21], (L, D_MODEL, 2 * D_FF), D_MODEL),
        "ffn2_w_out": dense(ks[22], (L, D_FF, D_MODEL), D_FF),
        "final_norm": gain(ks[23], (D_MODEL,)),
    }


def reference(x_prompt, x_sample, mem_prompt, mem_sample, ffn1_norm, ffn1_w_in, ffn1_w_out,
              mix_norm, w_in, mla_q_norm, mla_w_uq, mla_kv_norm, mla_w_ukv, attn_sink,
              mem_norm, w_mem_kv, w_branch_a, w_branch_b, w_branch_c, w_out,
              ffn2_norm, ffn2_w_in, ffn2_w_out, final_norm):
    layer_params = (ffn1_norm, ffn1_w_in, ffn1_w_out, mix_norm, w_in,
                    mla_q_norm, mla_w_uq, mla_kv_norm, mla_w_ukv, attn_sink,
                    mem_norm, w_mem_kv, w_branch_a, w_branch_b, w_branch_c, w_out,
                    ffn2_norm, ffn2_w_in, ffn2_w_out)
    y_prompt = trunk(x_prompt, mem_prompt, layer_params, final_norm)
    y_sample = trunk(x_sample, mem_sample, layer_params, final_norm)
    return (y_prompt, y_sample)
```

```python
import functools

import jax
import jax.numpy as jnp
from jax import lax
from jax.experimental import pallas as pl
from jax.experimental.pallas import tpu as pltpu

D_MODEL = 1024
N_MEM = 256
BLK = 128
WINDOW = 128
ROPE_THETA = 10000.0
EPS = 1e-6
NEG = -1e30
A_HEADS = 8
A_KV_HEADS = 2
A_HEAD_DIM = 64
A_Q = A_HEADS * A_HEAD_DIM
A_KV = A_KV_HEADS * A_HEAD_DIM
B_HEADS = 8
B_Q_LORA = 384
B_KV_LORA = 256
B_NOPE = 64
B_ROPE = 32
B_V = 64
B_OUT = B_HEADS * B_V
C_HEADS = 4
C_HEAD_DIM = 128
C_Q = C_HEADS * C_HEAD_DIM
N_BRANCH = 3
D_FF = 2816
IN_SIZES = (A_Q, A_KV, A_KV, B_Q_LORA, B_KV_LORA, B_ROPE, C_Q, N_BRANCH * D_MODEL)

LANES = 128
B_SLAB = LANES
B_CAT = B_HEADS * B_SLAB
OFF_QA, OFF_KA, OFF_VA, OFF_CQ, OFF_CKV, OFF_QC, OFF_KR = 0, 512, 640, 768, 1152, 1408, 1920
W1_COLS = 2048

TM_FFN = 512
FF_CHUNK = 512
TM_IN = 512
TA = 512
TQ = 512
TK = 1024
TKC = 256
TM_MERGE = 512
VMEM_LIMIT = 52 * 1024 * 1024

BF16 = jnp.bfloat16
F32 = jnp.float32
NT_DIMS = (((1,), (1,)), ((), ()))


def _rms(x, g):
    return x * lax.rsqrt(jnp.mean(x * x, axis=-1, keepdims=True) + EPS) * g


def _const_spec(shape):
    nd = len(shape)
    return pl.BlockSpec(shape, lambda *_: (0,) * nd, pipeline_mode=pl.Buffered(1))


def _params(sem):
    return pltpu.CompilerParams(dimension_semantics=sem, vmem_limit_bytes=VMEM_LIMIT)


def _ffn_kernel(x_ref, g_ref, w1_ref, w2_ref, fn_ref, o_ref, *, final_norm):
    x = x_ref[0]
    xn = _rms(x, g_ref[...]).astype(BF16)
    acc = jnp.zeros(x.shape, F32)
    lo = 0
    while lo < D_FF:
        fc = min(FF_CHUNK, D_FF - lo)
        g = jnp.dot(xn, w1_ref[:, lo:lo + fc], preferred_element_type=F32)
        u = jnp.dot(xn, w1_ref[:, D_FF + lo:D_FF + lo + fc], preferred_element_type=F32)
        h = (g * (1.0 / (1.0 + jnp.exp(-g))) * u).astype(BF16)
        acc = acc + jnp.dot(h, w2_ref[lo:lo + fc, :], preferred_element_type=F32)
        lo += fc
    y = x + 0.5 * acc
    if final_norm:
        y = _rms(y, fn_ref[...])
    o_ref[0] = y


def _ffn(x, g, w1, w2, fn, final_norm):
    B, S, D = x.shape
    return pl.pallas_call(
        functools.partial(_ffn_kernel, final_norm=final_norm),
        grid=(B, S // TM_FFN),
        in_specs=[
            pl.BlockSpec((1, TM_FFN, D), lambda b, i: (b, i, 0)),
            _const_spec((1, D)),
            _const_spec((D, 2 * D_FF)),
            _const_spec((D_FF, D)),
            _const_spec((1, D)),
        ],
        out_specs=pl.BlockSpec((1, TM_FFN, D), lambda b, i: (b, i, 0)),
        out_shape=jax.ShapeDtypeStruct((B, S, D), F32),
        compiler_params=_params(("parallel", "parallel")),
        name="ffn_final" if final_norm else "ffn",
    )(x, g, w1, w2, fn)


def _rope(v, tab_ref, half):
    return (v * tab_ref[0]
            + pltpu.roll(v, LANES - half, 1) * tab_ref[1]
            + pltpu.roll(v, half, 1) * tab_ref[2])


def _inproj_kernel(x_ref, g_ref, w1_ref, ta_ref, tb_ref, qn_ref, wuq_ref, kvn_ref, wk_ref, wvt_ref,
                   qa_ref, ka_ref, va_ref, qcat_ref, kcat_ref, vt_ref, qc_ref):
    u = _rms(x_ref[0], g_ref[...]).astype(BF16)
    z = jnp.dot(u, w1_ref[...], preferred_element_type=F32)

    rope_a = functools.partial(_rope, tab_ref=ta_ref, half=A_HEAD_DIM // 2)
    rope_b = functools.partial(_rope, tab_ref=tb_ref, half=B_ROPE // 2)

    for s in range(A_Q // LANES):
        sl = slice(OFF_QA + s * LANES, OFF_QA + (s + 1) * LANES)
        qa_ref[0, :, s * LANES:(s + 1) * LANES] = (rope_a(z[:, sl]) * (A_HEAD_DIM ** -0.5)).astype(BF16)
    ka_ref[0] = rope_a(z[:, OFF_KA:OFF_KA + A_KV]).astype(BF16)
    va_ref[0] = z[:, OFF_VA:OFF_VA + A_KV].astype(BF16)
    qc_ref[0] = z[:, OFF_QC:OFF_QC + C_Q].astype(BF16)

    cqn = _rms(z[:, OFF_CQ:OFF_CQ + B_Q_LORA], qn_ref[...]).astype(BF16)
    q = jnp.dot(cqn, wuq_ref[...], preferred_element_type=F32)
    scale = (B_NOPE + B_ROPE) ** -0.5
    for h in range(B_HEADS):
        sl = slice(h * B_SLAB, (h + 1) * B_SLAB)
        qcat_ref[0, :, sl] = (rope_b(q[:, sl]) * scale).astype(BF16)

    cn = _rms(z[:, OFF_CKV:OFF_CKV + B_KV_LORA], kvn_ref[...]).astype(BF16)
    kn = jnp.dot(cn, wk_ref[...], preferred_element_type=F32)
    kr = rope_b(z[:, OFF_KR:OFF_KR + LANES])
    for h in range(B_HEADS):
        sl = slice(h * B_SLAB, (h + 1) * B_SLAB)
        kcat_ref[0, :, sl] = (kn[:, sl] + kr).astype(BF16)

    vt = lax.dot_general(wvt_ref[...], cn, NT_DIMS, preferred_element_type=F32)
    for c in range(TM_IN // TKC):
        vt_ref[0, c] = vt[:, c * TKC:(c + 1) * TKC].astype(BF16)


def _inproj(x1, g, w1, tab_a, tab_b, qn, wuq, kvn, wk, wvt):
    B, S, D = x1.shape
    tok = lambda w: pl.BlockSpec((1, TM_IN, w), lambda b, i: (b, i, 0))
    tab = pl.BlockSpec((3, TM_IN, LANES), lambda b, i: (0, i, 0))
    out_shape = (
        jax.ShapeDtypeStruct((B, S, A_Q), BF16),
        jax.ShapeDtypeStruct((B, S, A_KV), BF16),
        jax.ShapeDtypeStruct((B, S, A_KV), BF16),
        jax.ShapeDtypeStruct((B, S, B_CAT), BF16),
        jax.ShapeDtypeStruct((B, S, B_CAT), BF16),
        jax.ShapeDtypeStruct((B, S // TKC, B_OUT, TKC), BF16),
        jax.ShapeDtypeStruct((B, S, C_Q), BF16),
    )
    out_specs = (
        tok(A_Q), tok(A_KV), tok(A_KV), tok(B_CAT), tok(B_CAT),
        pl.BlockSpec((1, TM_IN // TKC, B_OUT, TKC), lambda b, i: (b, i, 0, 0)),
        tok(C_Q),
    )
    return pl.pallas_call(
        _inproj_kernel,
        grid=(B, S // TM_IN),
        in_specs=[
            tok(D), _const_spec((1, D)), _const_spec((D, W1_COLS)), tab, tab,
            _const_spec((1, B_Q_LORA)), _const_spec((B_Q_LORA, B_CAT)),
            _const_spec((1, B_KV_LORA)), _const_spec((B_KV_LORA, B_CAT)),
            _const_spec((B_OUT, B_KV_LORA)),
        ],
        out_specs=out_specs,
        out_shape=out_shape,
        compiler_params=_params(("parallel", "parallel")),
        name="inproj",
    )(x1, g, w1, tab_a, tab_b, qn, wuq, kvn, wk, wvt)


def _attn_a_kernel(sink_ref, q_ref, kp_ref, kc_ref, kn_ref, vp_ref, vc_ref, vn_ref, o_ref, *, seq):
    i = pl.program_id(1)
    half = A_HEAD_DIM

    def variants(p_ref, c_ref, n_ref):
        full = jnp.concatenate([p_ref[0], c_ref[0], n_ref[0]], axis=0).astype(F32)
        swapped = pltpu.roll(full, half, 1)
        lo = lax.broadcasted_iota(jnp.int32, full.shape, 1) < half
        zero = jnp.zeros_like(full)
        return ((jnp.where(lo, full, zero).astype(BF16), jnp.where(lo, zero, swapped).astype(BF16)),
                (jnp.where(lo, swapped, zero).astype(BF16), jnp.where(lo, zero, full).astype(BF16)))

    kvar = variants(kp_ref, kc_ref, kn_ref)
    vvar = variants(vp_ref, vc_ref, vn_ref)

    r = lax.broadcasted_iota(jnp.int32, (BLK, 3 * BLK), 0)
    c = lax.broadcasted_iota(jnp.int32, (BLK, 3 * BLK), 1)
    band = jnp.abs(c - BLK - r) <= WINDOW

    for j in range(TA // BLK):
        kpos = i * TA + (j - 1) * BLK + c
        valid = band & (kpos >= 0) & (kpos < seq)
        rows = slice(j * BLK, (j + 1) * BLK)
        keys = slice(j * BLK, (j + 3) * BLK)
        for p in range(A_HEADS // 2):
            g = (2 * p) // (A_HEADS // A_KV_HEADS)
            q = q_ref[0, rows, p * LANES:(p + 1) * LANES]
            o = jnp.zeros((BLK, LANES), F32)
            for par in range(2):
                sink = sink_ref[2 * p + par]
                s = lax.dot_general(q, kvar[g][par][keys], NT_DIMS, preferred_element_type=F32)
                s = jnp.where(valid, s, NEG)
                m = jnp.maximum(jnp.max(s, axis=-1, keepdims=True), sink)
                e = jnp.exp(s - m)
                denom = jnp.sum(e, axis=-1, keepdims=True) + jnp.exp(sink - m)
                pv = jnp.dot(e.astype(BF16), vvar[g][par][keys], preferred_element_type=F32)
                o = o + pv / denom
            o_ref[0, rows, p * LANES:(p + 1) * LANES] = o.astype(BF16)


def _attn_a(qa, ka, va, sink):
    B, S, _ = qa.shape
    nb = S // BLK
    per = TA // BLK
    prev = pl.BlockSpec((1, BLK, A_KV), lambda b, i: (b, jnp.maximum(i * per - 1, 0), 0))
    cur = pl.BlockSpec((1, TA, A_KV), lambda b, i: (b, i, 0))
    nxt = pl.BlockSpec((1, BLK, A_KV), lambda b, i: (b, jnp.minimum((i + 1) * per, nb - 1), 0))
    return pl.pallas_call(
        functools.partial(_attn_a_kernel, seq=S),
        grid=(B, S // TA),
        in_specs=[
            pl.BlockSpec(memory_space=pltpu.SMEM),
            pl.BlockSpec((1, TA, A_Q), lambda b, i: (b, i, 0)),
            prev, cur, nxt, prev, cur, nxt,
        ],
        out_specs=pl.BlockSpec((1, TA, A_Q), lambda b, i: (b, i, 0)),
        out_shape=jax.ShapeDtypeStruct((B, S, A_Q), BF16),
        compiler_params=_params(("parallel", "parallel")),
        name="attn_a",
    )(sink, qa, ka, ka, ka, va, va, va)


def _mla_kernel(q_ref, k_ref, vt_ref, o_ref, m_ref, l_ref, acc_ref):
    j = pl.program_id(2)

    @pl.when(j == 0)
    def _():
        m_ref[...] = jnp.full(m_ref.shape, NEG, F32)
        l_ref[...] = jnp.zeros(l_ref.shape, F32)
        acc_ref[...] = jnp.zeros(acc_ref.shape, F32)

    def chunk(c, carry):
        rows = pl.ds(pl.multiple_of(c * TKC, TKC), TKC)
        for h in range(B_HEADS):
            sl = slice(h * B_SLAB, (h + 1) * B_SLAB)
            s = lax.dot_general(k_ref[0, rows, sl], q_ref[0, :, sl], NT_DIMS,
                                preferred_element_type=F32)
            m_old = m_ref[h]
            m_new = jnp.maximum(m_old, jnp.max(s, axis=0, keepdims=True))
            alpha = jnp.exp(m_old - m_new)
            p = jnp.exp(s - m_new)
            l_ref[h] = alpha * l_ref[h] + jnp.sum(p, axis=0, keepdims=True)
            pv = jnp.dot(vt_ref[0, c, h * B_V:(h + 1) * B_V, :], p.astype(BF16),
                         preferred_element_type=F32)
            acc_ref[h] = alpha * acc_ref[h] + pv
            m_ref[h] = m_new
        return carry

    lax.fori_loop(0, TK // TKC, chunk, 0)

    @pl.when(j == pl.num_programs(2) - 1)
    def _():
        for p in range(B_HEADS // 2):
            pair = jnp.concatenate([acc_ref[2 * p] / l_ref[2 * p],
                                    acc_ref[2 * p + 1] / l_ref[2 * p + 1]], axis=0)
            o_ref[0, :, p * LANES:(p + 1) * LANES] = pair.T.astype(BF16)


def _mla(qcat, kcat, vt):
    B, S, _ = qcat.shape
    return pl.pallas_call(
        _mla_kernel,
        grid=(B, S // TQ, S // TK),
        in_specs=[
            pl.BlockSpec((1, TQ, B_CAT), lambda b, i, j: (b, i, 0)),
            pl.BlockSpec((1, TK, B_CAT), lambda b, i, j: (b, j, 0)),
            pl.BlockSpec((1, TK // TKC, B_OUT, TKC), lambda b, i, j: (b, j, 0, 0)),
        ],
        out_specs=pl.BlockSpec((1, TQ, B_OUT), lambda b, i, j: (b, i, 0)),
        out_shape=jax.ShapeDtypeStruct((B, S, B_OUT), BF16),
        scratch_shapes=[
            pltpu.VMEM((B_HEADS, 1, TQ), F32),
            pltpu.VMEM((B_HEADS, 1, TQ), F32),
            pltpu.VMEM((B_HEADS, B_V, TQ), F32),
        ],
        compiler_params=_params(("parallel", "parallel", "arbitrary")),
        name="mla",
    )(qcat, kcat, vt)


def _memkv_kernel(mem_ref, g_ref, w_ref, k_ref, v_ref):
    mn = _rms(mem_ref[0], g_ref[...]).astype(BF16)
    kv = jnp.dot(mn, w_ref[...], preferred_element_type=F32)
    k_ref[0] = kv[:, :C_Q].astype(BF16)
    v_ref[0] = kv[:, C_Q:].astype(BF16)


def _memkv(mem, g, w):
    B = mem.shape[0]
    out = jax.ShapeDtypeStruct((B, N_MEM, C_Q), BF16)
    spec = pl.BlockSpec((1, N_MEM, C_Q), lambda b: (b, 0, 0))
    return pl.pallas_call(
        _memkv_kernel,
        grid=(B,),
        in_specs=[pl.BlockSpec((1, N_MEM, D_MODEL), lambda b: (b, 0, 0)),
                  _const_spec((1, D_MODEL)), _const_spec((D_MODEL, 2 * C_Q))],
        out_specs=(spec, spec),
        out_shape=(out, out),
        compiler_params=_params(("parallel",)),
        name="memkv",
    )(mem, g, w)


def _merge_kernel(x_ref, oa_ref, ob_ref, qc_ref, mk_ref, mv_ref, g_ref, wgl_ref, wa_ref, wb_ref, wc_ref,
                  wout_ref, o_ref):
    x = x_ref[0]
    u = _rms(x, g_ref[...]).astype(BF16)

    oc = []
    for h in range(C_HEADS):
        sl = slice(h * C_HEAD_DIM, (h + 1) * C_HEAD_DIM)
        s = lax.dot_general(qc_ref[0, :, sl], mk_ref[0, :, sl], NT_DIMS,
                            preferred_element_type=F32) * (C_HEAD_DIM ** -0.5)
        e = jnp.exp(s - jnp.max(s, axis=-1, keepdims=True))
        pv = jnp.dot(e.astype(BF16), mv_ref[0, :, sl], preferred_element_type=F32)
        oc.append((pv / jnp.sum(e, axis=-1, keepdims=True)).astype(BF16))
    oc = jnp.concatenate(oc, axis=1)

    branches = (
        jnp.dot(oa_ref[0], wa_ref[...], preferred_element_type=F32),
        jnp.dot(ob_ref[0], wb_ref[...], preferred_element_type=F32),
        jnp.dot(oc, wc_ref[...], preferred_element_type=F32),
    )
    merged = jnp.zeros(x.shape, F32)
    for i, br in enumerate(branches):
        gl = jnp.dot(u, wgl_ref[:, i * D_MODEL:(i + 1) * D_MODEL], preferred_element_type=F32)
        merged = merged + br * (1.0 / (1.0 + jnp.exp(-gl)))
    o_ref[0] = x + jnp.dot(merged.astype(BF16), wout_ref[...], preferred_element_type=F32)


def _merge(x1, oa, ob, qc, mk, mv, g, wgl, wa, wb, wc, wout):
    B, S, D = x1.shape
    tok = lambda w: pl.BlockSpec((1, TM_MERGE, w), lambda b, i: (b, i, 0))
    mem = pl.BlockSpec((1, N_MEM, C_Q), lambda b, i: (b, 0, 0))
    return pl.pallas_call(
        _merge_kernel,
        grid=(B, S // TM_MERGE),
        in_specs=[
            tok(D), tok(A_Q), tok(B_OUT), tok(C_Q), mem, mem,
            _const_spec((1, D)), _const_spec((D, N_BRANCH * D)),
            _const_spec((A_Q, D)), _const_spec((B_OUT, D)), _const_spec((C_Q, D)),
            _const_spec((D, D)),
        ],
        out_specs=tok(D),
        out_shape=jax.ShapeDtypeStruct((B, S, D), F32),
        compiler_params=_params(("parallel", "parallel")),
        name="merge",
    )(x1, oa, ob, qc, mk, mv, g, wgl, wa, wb, wc, wout)


def _rope_tables(seq, half, first_lane, period):
    inv = ROPE_THETA ** (-jnp.arange(0, 2 * half, 2, dtype=F32) / (2 * half))
    ang = jnp.arange(seq, dtype=F32)[:, None] * inv[None, :]
    cos, sin = jnp.cos(ang), jnp.sin(ang)
    lane = jnp.arange(LANES) % period - first_lane
    in_first = (lane >= 0) & (lane < half)
    in_second = (lane >= half) & (lane < 2 * half)
    idx = jnp.clip(jnp.where(in_second, lane - half, lane), 0, half - 1)
    cos_l, sin_l = cos[:, idx], sin[:, idx]
    c = jnp.where((in_first | in_second)[None, :], cos_l, 1.0)
    sa = jnp.where(in_first[None, :], -sin_l, 0.0)
    sb = jnp.where(in_second[None, :], sin_l, 0.0)
    return jnp.stack([c, sa, sb]).astype(F32)


def _prepare(ffn1_norm, ffn1_w_in, ffn1_w_out, mix_norm, w_in, mla_q_norm, mla_w_uq, mla_kv_norm,
             mla_w_ukv, attn_sink, mem_norm, w_mem_kv, w_branch_a, w_branch_b, w_branch_c, w_out,
             ffn2_norm, ffn2_w_in, ffn2_w_out, final_norm):
    row = lambda v: v.reshape(1, -1).astype(F32)
    cols, o = [], 0
    for s in IN_SIZES:
        cols.append(w_in[0][:, o:o + s])
        o += s
    w_qa, w_ka, w_va, w_cq, w_ckv, w_kr, w_qc, w_gl = cols
    w_kr_pad = jnp.zeros((D_MODEL, LANES), F32).at[:, B_NOPE:B_NOPE + B_ROPE].set(w_kr)
    w1 = jnp.concatenate([w_qa, w_ka, w_va, w_cq, w_ckv, w_qc, w_kr_pad], axis=1).astype(BF16)
    wuq = mla_w_uq[0].reshape(B_Q_LORA, B_HEADS, B_NOPE + B_ROPE)
    wuq = jnp.pad(wuq, ((0, 0), (0, 0), (0, B_SLAB - B_NOPE - B_ROPE))).reshape(B_Q_LORA, B_CAT).astype(BF16)
    wukv = mla_w_ukv[0].reshape(B_KV_LORA, B_HEADS, B_NOPE + B_V)
    wk = jnp.pad(wukv[:, :, :B_NOPE], ((0, 0), (0, 0), (0, B_SLAB - B_NOPE))).reshape(B_KV_LORA, B_CAT).astype(BF16)
    wvt = wukv[:, :, B_NOPE:].reshape(B_KV_LORA, B_OUT).T.astype(BF16)
    return dict(
        ffn1=(row(ffn1_norm[0]), ffn1_w_in[0].astype(BF16), ffn1_w_out[0].astype(BF16)),
        ffn2=(row(ffn2_norm[0]), ffn2_w_in[0].astype(BF16), ffn2_w_out[0].astype(BF16)),
        final=row(final_norm),
        inproj=(row(mix_norm[0]), w1),
        mla=(row(mla_q_norm[0]), wuq, row(mla_kv_norm[0]), wk, wvt),
        sink=attn_sink[0].astype(F32),
        mem=(row(mem_norm[0]), w_mem_kv[0].astype(BF16)),
        merge=(row(mix_norm[0]), w_gl.astype(BF16), w_branch_a[0].astype(BF16), w_branch_b[0].astype(BF16),
               w_branch_c[0].astype(BF16), w_out[0].astype(BF16)),
    )


def _trunk(x, mem, w):
    S = x.shape[1]
    tab_a = _rope_tables(S, A_HEAD_DIM // 2, 0, A_HEAD_DIM)
    tab_b = _rope_tables(S, B_ROPE // 2, B_NOPE, LANES)
    x1 = _ffn(x, *w["ffn1"], w["final"], final_norm=False)
    qa, ka, va, qcat, kcat, vt, qc = _inproj(x1, *w["inproj"], tab_a, tab_b, *w["mla"])
    oa = _attn_a(qa, ka, va, w["sink"])
    ob = _mla(qcat, kcat, vt)
    mk, mv = _memkv(mem, *w["mem"])
    x2 = _merge(x1, oa, ob, qc, mk, mv, *w["merge"])
    return _ffn(x2, *w["ffn2"], w["final"], final_norm=True)


def kernel(x_prompt, x_sample, mem_prompt, mem_sample, ffn1_norm, ffn1_w_in, ffn1_w_out, mix_norm, w_in,
           mla_q_norm, mla_w_uq, mla_kv_norm, mla_w_ukv, attn_sink, mem_norm, w_mem_kv, w_branch_a,
           w_branch_b, w_branch_c, w_out, ffn2_norm, ffn2_w_in, ffn2_w_out, final_norm):
    w = _prepare(ffn1_norm, ffn1_w_in, ffn1_w_out, mix_norm, w_in, mla_q_norm, mla_w_uq, mla_kv_norm,
                 mla_w_ukv, attn_sink, mem_norm, w_mem_kv, w_branch_a, w_branch_b, w_branch_c, w_out,
                 ffn2_norm, ffn2_w_in, ffn2_w_out, final_norm)
    return (_trunk(x_prompt, mem_prompt, w), _trunk(x_sample, mem_sample, w))
```

```python
import functools

import jax
import jax.numpy as jnp
from jax import lax
from jax.experimental import pallas as pl
from jax.experimental.pallas import tpu as pltpu

D_MODEL = 1024
N_MEM = 256
BLK = 128
WINDOW = 128
ROPE_THETA = 10000.0
EPS = 1e-6
NEG = -1e30
A_HEADS = 8
A_KV_HEADS = 2
A_HEAD_DIM = 64
A_Q = A_HEADS * A_HEAD_DIM
A_KV = A_KV_HEADS * A_HEAD_DIM
B_HEADS = 8
B_Q_LORA = 384
B_KV_LORA = 256
B_NOPE = 64
B_ROPE = 32
B_V = 64
B_OUT = B_HEADS * B_V
C_HEADS = 4
C_HEAD_DIM = 128
C_Q = C_HEADS * C_HEAD_DIM
N_BRANCH = 3
D_FF = 2816
IN_SIZES = (A_Q, A_KV, A_KV, B_Q_LORA, B_KV_LORA, B_ROPE, C_Q, N_BRANCH * D_MODEL)

LANES = 128
B_SLAB = LANES
B_CAT = B_HEADS * B_SLAB
OFF_QA, OFF_KA, OFF_VA, OFF_CQ, OFF_CKV, OFF_QC, OFF_KR = 0, 512, 640, 768, 1152, 1408, 1920
W1_COLS = 2048

TM_FFN = 512
FF_CHUNK = 512
TM_IN = 512
TA = 512
TQ = 512
TK = 2048
TKC = 512
V_EXT = 80
LOG2E = 1.4426950408889634
HEAD_SHIFT = B_HEADS.bit_length() - 1
assert 1 << HEAD_SHIFT == B_HEADS
TM_MERGE = 512
VMEM_LIMIT = 52 * 1024 * 1024

BF16 = jnp.bfloat16
F32 = jnp.float32
NT_DIMS = (((1,), (1,)), ((), ()))
TN_DIMS = (((0,), (0,)), ((), ()))


def _rms(x, g):
    return x * lax.rsqrt(jnp.mean(x * x, axis=-1, keepdims=True) + EPS) * g


def _const_spec(shape):
    nd = len(shape)
    return pl.BlockSpec(shape, lambda *_: (0,) * nd, pipeline_mode=pl.Buffered(1))


def _params(sem):
    return pltpu.CompilerParams(dimension_semantics=sem, vmem_limit_bytes=VMEM_LIMIT)


def _ffn_kernel(x_ref, g_ref, w1_ref, w2_ref, fn_ref, o_ref, *, final_norm):
    x = x_ref[0]
    xn = _rms(x, g_ref[...]).astype(BF16)
    acc = jnp.zeros(x.shape, F32)
    lo = 0
    while lo < D_FF:
        fc = min(FF_CHUNK, D_FF - lo)
        g = jnp.dot(xn, w1_ref[:, lo:lo + fc], preferred_element_type=F32)
        u = jnp.dot(xn, w1_ref[:, D_FF + lo:D_FF + lo + fc], preferred_element_type=F32)
        h = (g * (1.0 / (1.0 + jnp.exp(-g))) * u).astype(BF16)
        acc = acc + jnp.dot(h, w2_ref[lo:lo + fc, :], preferred_element_type=F32)
        lo += fc
    y = x + 0.5 * acc
    if final_norm:
        y = _rms(y, fn_ref[...])
    o_ref[0] = y


def _ffn(x, g, w1, w2, fn, final_norm):
    B, S, D = x.shape
    return pl.pallas_call(
        functools.partial(_ffn_kernel, final_norm=final_norm),
        grid=(B, S // TM_FFN),
        in_specs=[
            pl.BlockSpec((1, TM_FFN, D), lambda b, i: (b, i, 0)),
            _const_spec((1, D)),
            _const_spec((D, 2 * D_FF)),
            _const_spec((D_FF, D)),
            _const_spec((1, D)),
        ],
        out_specs=pl.BlockSpec((1, TM_FFN, D), lambda b, i: (b, i, 0)),
        out_shape=jax.ShapeDtypeStruct((B, S, D), F32),
        compiler_params=_params(("parallel", "parallel")),
        name="ffn_final" if final_norm else "ffn",
    )(x, g, w1, w2, fn)


def _rope(v, tab_ref, half, axis=1):
    return (v * tab_ref[0]
            + pltpu.roll(v, LANES - half, axis) * tab_ref[1]
            + pltpu.roll(v, half, axis) * tab_ref[2])


def _inproj_kernel(x_ref, g_ref, w1_ref, ta_ref, tb_ref, tbt_ref, qn_ref, wuqt_ref, kvn_ref, wk_ref, wvt_ref,
                   qa_ref, ka_ref, va_ref, qt_ref, kcat_ref, vt_ref, qc_ref):
    u = _rms(x_ref[0], g_ref[...]).astype(BF16)
    z = jnp.dot(u, w1_ref[...], preferred_element_type=F32)

    rope_a = functools.partial(_rope, tab_ref=ta_ref, half=A_HEAD_DIM // 2)
    rope_b = functools.partial(_rope, tab_ref=tb_ref, half=B_ROPE // 2)
    rope_bt = functools.partial(_rope, tab_ref=tbt_ref, half=B_ROPE // 2, axis=0)

    for s in range(A_Q // LANES):
        sl = slice(OFF_QA + s * LANES, OFF_QA + (s + 1) * LANES)
        qa_ref[0, :, s * LANES:(s + 1) * LANES] = (rope_a(z[:, sl]) * (A_HEAD_DIM ** -0.5)).astype(BF16)
    ka_ref[0] = rope_a(z[:, OFF_KA:OFF_KA + A_KV]).astype(BF16)
    va_ref[0] = z[:, OFF_VA:OFF_VA + A_KV].astype(BF16)
    qc_ref[0] = z[:, OFF_QC:OFF_QC + C_Q].astype(BF16)

    cqn = _rms(z[:, OFF_CQ:OFF_CQ + B_Q_LORA], qn_ref[...]).astype(BF16)
    qt = lax.dot_general(wuqt_ref[...], cqn, NT_DIMS, preferred_element_type=F32)
    scale = (B_NOPE + B_ROPE) ** -0.5 * LOG2E
    for h in range(B_HEADS):
        qt_ref[0, h] = (rope_bt(qt[h * B_SLAB:(h + 1) * B_SLAB, :]) * scale).astype(BF16)

    cn = _rms(z[:, OFF_CKV:OFF_CKV + B_KV_LORA], kvn_ref[...]).astype(BF16)
    kn = jnp.dot(cn, wk_ref[...], preferred_element_type=F32)
    kr = rope_b(z[:, OFF_KR:OFF_KR + LANES])
    for h in range(B_HEADS):
        kcat_ref[0, h] = (kn[:, h * B_SLAB:(h + 1) * B_SLAB] + kr).astype(BF16)

    vt = lax.dot_general(wvt_ref[...], cn, NT_DIMS, preferred_element_type=F32)
    ones_row = (lax.broadcasted_iota(jnp.int32, (V_EXT - B_V, TKC), 0) == 0).astype(BF16)
    for c in range(TM_IN // TKC):
        for h in range(B_HEADS):
            vt_ref[0, c, h, 0:B_V, :] = vt[h * B_V:(h + 1) * B_V, c * TKC:(c + 1) * TKC].astype(BF16)
            vt_ref[0, c, h, B_V:V_EXT, :] = ones_row


def _inproj(x1, g, w1, tab_a, tab_b, tab_bt, qn, wuqt, kvn, wk, wvt):
    B, S, D = x1.shape
    tok = lambda w: pl.BlockSpec((1, TM_IN, w), lambda b, i: (b, i, 0))
    tab = pl.BlockSpec((3, TM_IN, LANES), lambda b, i: (0, i, 0))
    tab_t = pl.BlockSpec((3, LANES, TM_IN), lambda b, i: (0, 0, i))
    out_shape = (
        jax.ShapeDtypeStruct((B, S, A_Q), BF16),
        jax.ShapeDtypeStruct((B, S, A_KV), BF16),
        jax.ShapeDtypeStruct((B, S, A_KV), BF16),
        jax.ShapeDtypeStruct((B, B_HEADS, B_SLAB, S), BF16),
        jax.ShapeDtypeStruct((B, B_HEADS, S, B_SLAB), BF16),
        jax.ShapeDtypeStruct((B, S // TKC, B_HEADS, V_EXT, TKC), BF16),
        jax.ShapeDtypeStruct((B, S, C_Q), BF16),
    )
    out_specs = (
        tok(A_Q), tok(A_KV), tok(A_KV),
        pl.BlockSpec((1, B_HEADS, B_SLAB, TM_IN), lambda b, i: (b, 0, 0, i)),
        pl.BlockSpec((1, B_HEADS, TM_IN, B_SLAB), lambda b, i: (b, 0, i, 0)),
        pl.BlockSpec((1, TM_IN // TKC, B_HEADS, V_EXT, TKC), lambda b, i: (b, i, 0, 0, 0)),
        tok(C_Q),
    )
    return pl.pallas_call(
        _inproj_kernel,
        grid=(B, S // TM_IN),
        in_specs=[
            tok(D), _const_spec((1, D)), _const_spec((D, W1_COLS)), tab, tab, tab_t,
            _const_spec((1, B_Q_LORA)), _const_spec((B_CAT, B_Q_LORA)),
            _const_spec((1, B_KV_LORA)), _const_spec((B_KV_LORA, B_CAT)),
            _const_spec((B_OUT, B_KV_LORA)),
        ],
        out_specs=out_specs,
        out_shape=out_shape,
        compiler_params=_params(("parallel", "parallel")),
        name="inproj",
    )(x1, g, w1, tab_a, tab_b, tab_bt, qn, wuqt, kvn, wk, wvt)


def _attn_a_kernel(sink_ref, q_ref, kp_ref, kc_ref, kn_ref, vp_ref, vc_ref, vn_ref, o_ref, *, seq):
    i = pl.program_id(1)
    half = A_HEAD_DIM

    def variants(p_ref, c_ref, n_ref):
        full = jnp.concatenate([p_ref[0], c_ref[0], n_ref[0]], axis=0).astype(F32)
        swapped = pltpu.roll(full, half, 1)
        lo = lax.broadcasted_iota(jnp.int32, full.shape, 1) < half
        zero = jnp.zeros_like(full)
        return ((jnp.where(lo, full, zero).astype(BF16), jnp.where(lo, zero, swapped).astype(BF16)),
                (jnp.where(lo, swapped, zero).astype(BF16), jnp.where(lo, zero, full).astype(BF16)))

    kvar = variants(kp_ref, kc_ref, kn_ref)
    vvar = variants(vp_ref, vc_ref, vn_ref)

    r = lax.broadcasted_iota(jnp.int32, (BLK, 3 * BLK), 0)
    c = lax.broadcasted_iota(jnp.int32, (BLK, 3 * BLK), 1)
    band = jnp.abs(c - BLK - r) <= WINDOW

    for j in range(TA // BLK):
        kpos = i * TA + (j - 1) * BLK + c
        valid = band & (kpos >= 0) & (kpos < seq)
        rows = slice(j * BLK, (j + 1) * BLK)
        keys = slice(j * BLK, (j + 3) * BLK)
        for p in range(A_HEADS // 2):
            g = (2 * p) // (A_HEADS // A_KV_HEADS)
            q = q_ref[0, rows, p * LANES:(p + 1) * LANES]
            o = jnp.zeros((BLK, LANES), F32)
            for par in range(2):
                sink = sink_ref[2 * p + par]
                s = lax.dot_general(q, kvar[g][par][keys], NT_DIMS, preferred_element_type=F32)
                s = jnp.where(valid, s, NEG)
                m = jnp.maximum(jnp.max(s, axis=-1, keepdims=True), sink)
                e = jnp.exp(s - m)
                denom = jnp.sum(e, axis=-1, keepdims=True) + jnp.exp(sink - m)
                pv = jnp.dot(e.astype(BF16), vvar[g][par][keys], preferred_element_type=F32)
                o = o + pv / denom
            o_ref[0, rows, p * LANES:(p + 1) * LANES] = o.astype(BF16)


def _attn_a(qa, ka, va, sink):
    B, S, _ = qa.shape
    nb = S // BLK
    per = TA // BLK
    prev = pl.BlockSpec((1, BLK, A_KV), lambda b, i: (b, jnp.maximum(i * per - 1, 0), 0))
    cur = pl.BlockSpec((1, TA, A_KV), lambda b, i: (b, i, 0))
    nxt = pl.BlockSpec((1, BLK, A_KV), lambda b, i: (b, jnp.minimum((i + 1) * per, nb - 1), 0))
    return pl.pallas_call(
        functools.partial(_attn_a_kernel, seq=S),
        grid=(B, S // TA),
        in_specs=[
            pl.BlockSpec(memory_space=pltpu.SMEM),
            pl.BlockSpec((1, TA, A_Q), lambda b, i: (b, i, 0)),
            prev, cur, nxt, prev, cur, nxt,
        ],
        out_specs=pl.BlockSpec((1, TA, A_Q), lambda b, i: (b, i, 0)),
        out_shape=jax.ShapeDtypeStruct((B, S, A_Q), BF16),
        compiler_params=_params(("parallel", "parallel")),
        name="attn_a",
    )(sink, qa, ka, ka, ka, va, va, va)


def _mla_kernel(qt_ref, k_ref, vt_ref, o_ref, m_ref, acc_ref, s_ref):
    j = pl.program_id(2)
    n_items = (TK // TKC) * B_HEADS

    @pl.when(j == 0)
    def _():
        m_ref[...] = jnp.full(m_ref.shape, NEG, F32)
        acc_ref[...] = jnp.zeros(acc_ref.shape, F32)

    def split(item):
        return lax.shift_right_logical(item, HEAD_SHIFT), item & (B_HEADS - 1)

    def scores(item):
        c, h = split(item)
        rows = pl.ds(pl.multiple_of(c * TKC, TKC), TKC)
        return jnp.dot(k_ref[0, h, rows, :], qt_ref[0, h], preferred_element_type=F32)

    def consume(item, s):
        c, h = split(item)
        m_old = m_ref[h]
        m_new = jnp.maximum(m_old, jnp.max(s, axis=0, keepdims=True))
        p = jnp.exp2(s - m_new).astype(BF16)
        pv = jnp.dot(vt_ref[0, c, h], p, preferred_element_type=F32)
        acc_ref[h] = jnp.exp2(m_old - m_new) * acc_ref[h] + pv
        m_ref[h] = m_new

    s_ref[0] = scores(0)

    def pair(t, carry):
        i0 = 2 * t
        s_ref[1] = scores(i0 + 1)
        consume(i0, s_ref[0])
        s_ref[0] = scores(jnp.minimum(i0 + 2, n_items - 1))
        consume(i0 + 1, s_ref[1])
        return carry

    lax.fori_loop(0, n_items // 2, pair, 0)

    @pl.when(j == pl.num_programs(2) - 1)
    def _():
        for h in range(B_HEADS):
            o_ref[0, h] = (acc_ref[h, 0:B_V, :] / acc_ref[h, B_V:B_V + 1, :]).astype(BF16)


def _mla(qt, kcat, vt):
    B, _, _, S = qt.shape
    return pl.pallas_call(
        _mla_kernel,
        grid=(B, S // TQ, S // TK),
        in_specs=[
            pl.BlockSpec((1, B_HEADS, B_SLAB, TQ), lambda b, i, j: (b, 0, 0, i)),
            pl.BlockSpec((1, B_HEADS, TK, B_SLAB), lambda b, i, j: (b, 0, j, 0)),
            pl.BlockSpec((1, TK // TKC, B_HEADS, V_EXT, TKC), lambda b, i, j: (b, j, 0, 0, 0)),
        ],
        out_specs=pl.BlockSpec((1, B_HEADS, B_V, TQ), lambda b, i, j: (b, 0, 0, i)),
        out_shape=jax.ShapeDtypeStruct((B, B_HEADS, B_V, S), BF16),
        scratch_shapes=[
            pltpu.VMEM((B_HEADS, 1, TQ), F32),
            pltpu.VMEM((B_HEADS, V_EXT, TQ), F32),
            pltpu.VMEM((2, TKC, TQ), F32),
        ],
        compiler_params=_params(("parallel", "parallel", "arbitrary")),
        name="mla",
    )(qt, kcat, vt)


def _memkv_kernel(mem_ref, g_ref, w_ref, k_ref, v_ref):
    mn = _rms(mem_ref[0], g_ref[...]).astype(BF16)
    kv = jnp.dot(mn, w_ref[...], preferred_element_type=F32)
    k_ref[0] = kv[:, :C_Q].astype(BF16)
    v_ref[0] = kv[:, C_Q:].astype(BF16)


def _memkv(mem, g, w):
    B = mem.shape[0]
    out = jax.ShapeDtypeStruct((B, N_MEM, C_Q), BF16)
    spec = pl.BlockSpec((1, N_MEM, C_Q), lambda b: (b, 0, 0))
    return pl.pallas_call(
        _memkv_kernel,
        grid=(B,),
        in_specs=[pl.BlockSpec((1, N_MEM, D_MODEL), lambda b: (b, 0, 0)),
                  _const_spec((1, D_MODEL)), _const_spec((D_MODEL, 2 * C_Q))],
        out_specs=(spec, spec),
        out_shape=(out, out),
        compiler_params=_params(("parallel",)),
        name="memkv",
    )(mem, g, w)


def _merge_kernel(x_ref, oa_ref, obt_ref, qc_ref, mk_ref, mv_ref, g_ref, wgl_ref, wa_ref, wb_ref, wc_ref,
                  wout_ref, o_ref):
    x = x_ref[0]
    u = _rms(x, g_ref[...]).astype(BF16)

    oc = []
    for h in range(C_HEADS):
        sl = slice(h * C_HEAD_DIM, (h + 1) * C_HEAD_DIM)
        s = lax.dot_general(qc_ref[0, :, sl], mk_ref[0, :, sl], NT_DIMS,
                            preferred_element_type=F32) * (C_HEAD_DIM ** -0.5)
        e = jnp.exp(s - jnp.max(s, axis=-1, keepdims=True))
        pv = jnp.dot(e.astype(BF16), mv_ref[0, :, sl], preferred_element_type=F32)
        oc.append((pv / jnp.sum(e, axis=-1, keepdims=True)).astype(BF16))
    oc = jnp.concatenate(oc, axis=1)

    ob_t = obt_ref[0].reshape(B_OUT, x.shape[0])
    branches = (
        jnp.dot(oa_ref[0], wa_ref[...], preferred_element_type=F32),
        lax.dot_general(ob_t, wb_ref[...], TN_DIMS, preferred_element_type=F32),
        jnp.dot(oc, wc_ref[...], preferred_element_type=F32),
    )
    merged = jnp.zeros(x.shape, F32)
    for i, br in enumerate(branches):
        gl = jnp.dot(u, wgl_ref[:, i * D_MODEL:(i + 1) * D_MODEL], preferred_element_type=F32)
        merged = merged + br * (1.0 / (1.0 + jnp.exp(-gl)))
    o_ref[0] = x + jnp.dot(merged.astype(BF16), wout_ref[...], preferred_element_type=F32)


def _merge(x1, oa, obt, qc, mk, mv, g, wgl, wa, wb, wc, wout):
    B, S, D = x1.shape
    tok = lambda w: pl.BlockSpec((1, TM_MERGE, w), lambda b, i: (b, i, 0))
    mem = pl.BlockSpec((1, N_MEM, C_Q), lambda b, i: (b, 0, 0))
    return pl.pallas_call(
        _merge_kernel,
        grid=(B, S // TM_MERGE),
        in_specs=[
            tok(D), tok(A_Q),
            pl.BlockSpec((1, B_HEADS, B_V, TM_MERGE), lambda b, i: (b, 0, 0, i)),
            tok(C_Q), mem, mem,
            _const_spec((1, D)), _const_spec((D, N_BRANCH * D)),
            _const_spec((A_Q, D)), _const_spec((B_OUT, D)), _const_spec((C_Q, D)),
            _const_spec((D, D)),
        ],
        out_specs=tok(D),
        out_shape=jax.ShapeDtypeStruct((B, S, D), F32),
        compiler_params=_params(("parallel", "parallel")),
        name="merge",
    )(x1, oa, obt, qc, mk, mv, g, wgl, wa, wb, wc, wout)


def _rope_tables(seq, half, first_lane, period):
    inv = ROPE_THETA ** (-jnp.arange(0, 2 * half, 2, dtype=F32) / (2 * half))
    ang = jnp.arange(seq, dtype=F32)[:, None] * inv[None, :]
    cos, sin = jnp.cos(ang), jnp.sin(ang)
    lane = jnp.arange(LANES) % period - first_lane
    in_first = (lane >= 0) & (lane < half)
    in_second = (lane >= half) & (lane < 2 * half)
    idx = jnp.clip(jnp.where(in_second, lane - half, lane), 0, half - 1)
    cos_l, sin_l = cos[:, idx], sin[:, idx]
    c = jnp.where((in_first | in_second)[None, :], cos_l, 1.0)
    sa = jnp.where(in_first[None, :], -sin_l, 0.0)
    sb = jnp.where(in_second[None, :], sin_l, 0.0)
    return jnp.stack([c, sa, sb]).astype(F32)


def _prepare(ffn1_norm, ffn1_w_in, ffn1_w_out, mix_norm, w_in, mla_q_norm, mla_w_uq, mla_kv_norm,
             mla_w_ukv, attn_sink, mem_norm, w_mem_kv, w_branch_a, w_branch_b, w_branch_c, w_out,
             ffn2_norm, ffn2_w_in, ffn2_w_out, final_norm):
    row = lambda v: v.reshape(1, -1).astype(F32)
    cols, o = [], 0
    for s in IN_SIZES:
        cols.append(w_in[0][:, o:o + s])
        o += s
    w_qa, w_ka, w_va, w_cq, w_ckv, w_kr, w_qc, w_gl = cols
    w_kr_pad = jnp.zeros((D_MODEL, LANES), F32).at[:, B_NOPE:B_NOPE + B_ROPE].set(w_kr)
    w1 = jnp.concatenate([w_qa, w_ka, w_va, w_cq, w_ckv, w_qc, w_kr_pad], axis=1).astype(BF16)
    wuq = mla_w_uq[0].reshape(B_Q_LORA, B_HEADS, B_NOPE + B_ROPE)
    wuqt = jnp.pad(wuq, ((0, 0), (0, 0), (0, B_SLAB - B_NOPE - B_ROPE))).reshape(B_Q_LORA, B_CAT).T.astype(BF16)
    wukv = mla_w_ukv[0].reshape(B_KV_LORA, B_HEADS, B_NOPE + B_V)
    wk = jnp.pad(wukv[:, :, :B_NOPE], ((0, 0), (0, 0), (0, B_SLAB - B_NOPE))).reshape(B_KV_LORA, B_CAT).astype(BF16)
    wvt = wukv[:, :, B_NOPE:].reshape(B_KV_LORA, B_OUT).T.astype(BF16)
    return dict(
        ffn1=(row(ffn1_norm[0]), ffn1_w_in[0].astype(BF16), ffn1_w_out[0].astype(BF16)),
        ffn2=(row(ffn2_norm[0]), ffn2_w_in[0].astype(BF16), ffn2_w_out[0].astype(BF16)),
        final=row(final_norm),
        inproj=(row(mix_norm[0]), w1),
        mla=(row(mla_q_norm[0]), wuqt, row(mla_kv_norm[0]), wk, wvt),
        sink=attn_sink[0].astype(F32),
        mem=(row(mem_norm[0]), w_mem_kv[0].astype(BF16)),
        merge=(row(mix_norm[0]), w_gl.astype(BF16), w_branch_a[0].astype(BF16), w_branch_b[0].astype(BF16),
               w_branch_c[0].astype(BF16), w_out[0].astype(BF16)),
    )


def _trunk(x, mem, w):
    S = x.shape[1]
    tab_a = _rope_tables(S, A_HEAD_DIM // 2, 0, A_HEAD_DIM)
    tab_b = _rope_tables(S, B_ROPE // 2, B_NOPE, LANES)
    x1 = _ffn(x, *w["ffn1"], w["final"], final_norm=False)
    tab_bt = jnp.swapaxes(tab_b, 1, 2)
    qa, ka, va, qt, kcat, vt, qc = _inproj(x1, *w["inproj"], tab_a, tab_b, tab_bt, *w["mla"])
    oa = _attn_a(qa, ka, va, w["sink"])
    obt = _mla(qt, kcat, vt)
    mk, mv = _memkv(mem, *w["mem"])
    x2 = _merge(x1, oa, obt, qc, mk, mv, *w["merge"])
    return _ffn(x2, *w["ffn2"], w["final"], final_norm=True)


def kernel(x_prompt, x_sample, mem_prompt, mem_sample, ffn1_norm, ffn1_w_in, ffn1_w_out, mix_norm, w_in,
           mla_q_norm, mla_w_uq, mla_kv_norm, mla_w_ukv, attn_sink, mem_norm, w_mem_kv, w_branch_a,
           w_branch_b, w_branch_c, w_out, ffn2_norm, ffn2_w_in, ffn2_w_out, final_norm):
    w = _prepare(ffn1_norm, ffn1_w_in, ffn1_w_out, mix_norm, w_in, mla_q_norm, mla_w_uq, mla_kv_norm,
                 mla_w_ukv, attn_sink, mem_norm, w_mem_kv, w_branch_a, w_branch_b, w_branch_c, w_out,
                 ffn2_norm, ffn2_w_in, ffn2_w_out, final_norm)
    return (_trunk(x_prompt, mem_prompt, w), _trunk(x_sample, mem_sample, w))
```

```python
import functools

import jax
import jax.numpy as jnp
from jax import lax
from jax.experimental import pallas as pl
from jax.experimental.pallas import tpu as pltpu

D_MODEL = 1024
N_MEM = 256
BLK = 128
WINDOW = 128
ROPE_THETA = 10000.0
EPS = 1e-6
NEG = -1e30
A_HEADS = 8
A_KV_HEADS = 2
A_GROUP = A_HEADS // A_KV_HEADS
A_HEAD_DIM = 64
A_Q = A_HEADS * A_HEAD_DIM
A_KV = A_KV_HEADS * A_HEAD_DIM
B_HEADS = 8
B_Q_LORA = 384
B_KV_LORA = 256
B_NOPE = 64
B_ROPE = 32
B_V = 64
B_OUT = B_HEADS * B_V
C_HEADS = 4
C_HEAD_DIM = 128
C_Q = C_HEADS * C_HEAD_DIM
N_BRANCH = 3
D_FF = 2816
IN_SIZES = (A_Q, A_KV, A_KV, B_Q_LORA, B_KV_LORA, B_ROPE, C_Q, N_BRANCH * D_MODEL)

LANES = 128
B_SLAB = LANES
B_CAT = B_HEADS * B_SLAB
OFF_KA, OFF_CQ, OFF_CKV, OFF_QC, OFF_KR = 0, 128, 512, 768, 1280
W1_COLS = 1408
WT_ROWS = A_Q + A_KV

TM_FFN = 512
FF_CHUNK = 512
TM_IN = 512
TA = 512
TQ = 512
TK = 2048
TKC = 256
MLA_DEPTH = 3
V_EXT = 80
TM_MERGE = 512
VMEM_LIMIT = 52 * 1024 * 1024
LOG2E = 1.4426950408889634

BF16 = jnp.bfloat16
F32 = jnp.float32
NT_DIMS = (((1,), (1,)), ((), ()))
TN_DIMS = (((0,), (0,)), ((), ()))


def _rms(x, g):
    return x * lax.rsqrt(jnp.mean(x * x, axis=-1, keepdims=True) + EPS) * g


def _const_spec(shape):
    nd = len(shape)
    return pl.BlockSpec(shape, lambda *_: (0,) * nd, pipeline_mode=pl.Buffered(1))


def _params(sem):
    return pltpu.CompilerParams(dimension_semantics=sem, vmem_limit_bytes=VMEM_LIMIT)


def _ones_row_tile(width):
    return (lax.broadcasted_iota(jnp.int32, (V_EXT - B_V, width), 0) == 0).astype(BF16)


def _ffn_kernel(x_ref, g_ref, w1_ref, w2_ref, fn_ref, o_ref, *, final_norm):
    x = x_ref[0]
    xn = _rms(x, g_ref[...]).astype(BF16)
    acc = jnp.zeros(x.shape, F32)
    lo = 0
    while lo < D_FF:
        fc = min(FF_CHUNK, D_FF - lo)
        g = jnp.dot(xn, w1_ref[:, lo:lo + fc], preferred_element_type=F32)
        u = jnp.dot(xn, w1_ref[:, D_FF + lo:D_FF + lo + fc], preferred_element_type=F32)
        h = (g * (1.0 / (1.0 + jnp.exp(-g))) * u).astype(BF16)
        acc = acc + jnp.dot(h, w2_ref[lo:lo + fc, :], preferred_element_type=F32)
        lo += fc
    y = x + 0.5 * acc
    if final_norm:
        y = _rms(y, fn_ref[...])
    o_ref[0] = y


def _ffn(x, g, w1, w2, fn, final_norm):
    B, S, D = x.shape
    return pl.pallas_call(
        functools.partial(_ffn_kernel, final_norm=final_norm),
        grid=(B, S // TM_FFN),
        in_specs=[
            pl.BlockSpec((1, TM_FFN, D), lambda b, i: (b, i, 0)),
            _const_spec((1, D)),
            _const_spec((D, 2 * D_FF)),
            _const_spec((D_FF, D)),
            _const_spec((1, D)),
        ],
        out_specs=pl.BlockSpec((1, TM_FFN, D), lambda b, i: (b, i, 0)),
        out_shape=jax.ShapeDtypeStruct((B, S, D), F32),
        compiler_params=_params(("parallel", "parallel")),
        name="ffn_final" if final_norm else "ffn",
    )(x, g, w1, w2, fn)


def _rope(v, tab_ref, half, width, axis):
    return (v * tab_ref[0]
            + pltpu.roll(v, width - half, axis) * tab_ref[1]
            + pltpu.roll(v, half, axis) * tab_ref[2])


def _inproj_kernel(x_ref, g_ref, w1_ref, wt_ref, ta_ref, tat_ref, tb_ref, tbt_ref, qn_ref, wuqt_ref, kvn_ref,
                   wk_ref, wvt_ref, qat_ref, ka_ref, vat_ref, qt_ref, kcat_ref, vt_ref, qc_ref):
    tm = x_ref.shape[1]
    u = _rms(x_ref[0], g_ref[...]).astype(BF16)
    z = jnp.dot(u, w1_ref[...], preferred_element_type=F32)
    zt = lax.dot_general(wt_ref[...], u, NT_DIMS, preferred_element_type=F32)

    rope_a = functools.partial(_rope, tab_ref=ta_ref, half=A_HEAD_DIM // 2, width=LANES, axis=1)
    rope_at = functools.partial(_rope, tab_ref=tat_ref, half=A_HEAD_DIM // 2, width=A_HEAD_DIM, axis=0)
    rope_b = functools.partial(_rope, tab_ref=tb_ref, half=B_ROPE // 2, width=LANES, axis=1)
    rope_bt = functools.partial(_rope, tab_ref=tbt_ref, half=B_ROPE // 2, width=LANES, axis=0)

    scale_a = A_HEAD_DIM ** -0.5 * LOG2E
    zeros_half = jnp.zeros((A_HEAD_DIM, tm), BF16)
    for h in range(A_HEADS):
        q = (rope_at(zt[h * A_HEAD_DIM:(h + 1) * A_HEAD_DIM, :]) * scale_a).astype(BF16)
        g_kv = h // A_GROUP
        qat_ref[0, h, g_kv * A_HEAD_DIM:(g_kv + 1) * A_HEAD_DIM, :] = q
        qat_ref[0, h, (1 - g_kv) * A_HEAD_DIM:(2 - g_kv) * A_HEAD_DIM, :] = zeros_half
    ka_ref[0] = rope_a(z[:, OFF_KA:OFF_KA + A_KV]).astype(BF16)
    ones_a = _ones_row_tile(tm)
    for g_kv in range(A_KV_HEADS):
        vat_ref[0, g_kv, 0:A_HEAD_DIM, :] = zt[A_Q + g_kv * A_HEAD_DIM:A_Q + (g_kv + 1) * A_HEAD_DIM, :].astype(BF16)
        vat_ref[0, g_kv, A_HEAD_DIM:V_EXT, :] = ones_a

    qc_ref[0] = z[:, OFF_QC:OFF_QC + C_Q].astype(BF16)

    cqn = _rms(z[:, OFF_CQ:OFF_CQ + B_Q_LORA], qn_ref[...]).astype(BF16)
    qt = lax.dot_general(wuqt_ref[...], cqn, NT_DIMS, preferred_element_type=F32)
    scale_b = (B_NOPE + B_ROPE) ** -0.5 * LOG2E
    for h in range(B_HEADS):
        qt_ref[0, h] = (rope_bt(qt[h * B_SLAB:(h + 1) * B_SLAB, :]) * scale_b).astype(BF16)

    cn = _rms(z[:, OFF_CKV:OFF_CKV + B_KV_LORA], kvn_ref[...]).astype(BF16)
    kn = jnp.dot(cn, wk_ref[...], preferred_element_type=F32)
    kr = rope_b(z[:, OFF_KR:OFF_KR + LANES])
    for h in range(B_HEADS):
        kcat_ref[0, h] = (kn[:, h * B_SLAB:(h + 1) * B_SLAB] + kr).astype(BF16)

    vt = lax.dot_general(wvt_ref[...], cn, NT_DIMS, preferred_element_type=F32)
    ones_b = _ones_row_tile(TKC)
    for c in range(tm // TKC):
        for h in range(B_HEADS):
            vt_ref[0, c, h, 0:B_V, :] = vt[h * B_V:(h + 1) * B_V, c * TKC:(c + 1) * TKC].astype(BF16)
            vt_ref[0, c, h, B_V:V_EXT, :] = ones_b


def _inproj(x1, g, w1, wt, tab_a, tab_at, tab_b, tab_bt, qn, wuqt, kvn, wk, wvt):
    B, S, D = x1.shape
    tok = lambda w: pl.BlockSpec((1, TM_IN, w), lambda b, i: (b, i, 0))
    tab = pl.BlockSpec((3, TM_IN, LANES), lambda b, i: (0, i, 0))
    tab_t = lambda rows: pl.BlockSpec((3, rows, TM_IN), lambda b, i: (0, 0, i))
    head_t = lambda heads, rows: pl.BlockSpec((1, heads, rows, TM_IN), lambda b, i: (b, 0, 0, i))
    out_shape = (
        jax.ShapeDtypeStruct((B, A_HEADS, LANES, S), BF16),
        jax.ShapeDtypeStruct((B, S, A_KV), BF16),
        jax.ShapeDtypeStruct((B, A_KV_HEADS, V_EXT, S), BF16),
        jax.ShapeDtypeStruct((B, B_HEADS, B_SLAB, S), BF16),
        jax.ShapeDtypeStruct((B, B_HEADS, S, B_SLAB), BF16),
        jax.ShapeDtypeStruct((B, S // TKC, B_HEADS, V_EXT, TKC), BF16),
        jax.ShapeDtypeStruct((B, S, C_Q), BF16),
    )
    out_specs = (
        head_t(A_HEADS, LANES), tok(A_KV), head_t(A_KV_HEADS, V_EXT), head_t(B_HEADS, B_SLAB),
        pl.BlockSpec((1, B_HEADS, TM_IN, B_SLAB), lambda b, i: (b, 0, i, 0)),
        pl.BlockSpec((1, TM_IN // TKC, B_HEADS, V_EXT, TKC), lambda b, i: (b, i, 0, 0, 0)),
        tok(C_Q),
    )
    return pl.pallas_call(
        _inproj_kernel,
        grid=(B, S // TM_IN),
        in_specs=[
            tok(D), _const_spec((1, D)), _const_spec((D, W1_COLS)), _const_spec((WT_ROWS, D)),
            tab, tab_t(A_HEAD_DIM), tab, tab_t(LANES),
            _const_spec((1, B_Q_LORA)), _const_spec((B_CAT, B_Q_LORA)),
            _const_spec((1, B_KV_LORA)), _const_spec((B_KV_LORA, B_CAT)),
            _const_spec((B_OUT, B_KV_LORA)),
        ],
        out_specs=out_specs,
        out_shape=out_shape,
        compiler_params=_params(("parallel", "parallel")),
        name="inproj",
    )(x1, g, w1, wt, tab_a, tab_at, tab_b, tab_bt, qn, wuqt, kvn, wk, wvt)


def _attn_a_kernel(sink_ref, qt_ref, kp_ref, kc_ref, kn_ref, vp_ref, vc_ref, vn_ref, o_ref, *, seq):
    i = pl.program_id(1)
    k_all = jnp.concatenate([kp_ref[0], kc_ref[0], kn_ref[0]], axis=0)
    v_all = [jnp.concatenate([vp_ref[0, g], vc_ref[0, g], vn_ref[0, g]], axis=1)
             for g in range(A_KV_HEADS)]

    per = TA // BLK
    width = A_GROUP * BLK
    c = lax.broadcasted_iota(jnp.int32, (3 * BLK, width), 0)
    r = lax.broadcasted_iota(jnp.int32, (3 * BLK, width), 1) & (BLK - 1)
    bias_mid = jnp.where(jnp.abs(c - BLK - r) <= WINDOW, 0.0, NEG)
    bias_first = jnp.where(i * TA - BLK + c >= 0, bias_mid, NEG)
    bias_last = jnp.where(i * TA + (per - 2) * BLK + c < seq, bias_mid, NEG)
    biases = [bias_first] + [bias_mid] * (per - 2) + [bias_last]

    for j in range(per):
        cols = slice(j * BLK, (j + 1) * BLK)
        win = slice(j * BLK, (j + 3) * BLK)
        for g in range(A_KV_HEADS):
            qt = jnp.concatenate([qt_ref[0, g * A_GROUP + hh, :, cols] for hh in range(A_GROUP)], axis=1)
            s = jnp.dot(k_all[win], qt, preferred_element_type=F32) + biases[j]
            sink = sink_ref[g]
            m = jnp.maximum(jnp.max(s, axis=0, keepdims=True), sink)
            p = jnp.exp2(s - m).astype(BF16)
            pv = jnp.dot(v_all[g][:, win], p, preferred_element_type=F32)
            denom = pv[A_HEAD_DIM:A_HEAD_DIM + 1, :] + jnp.exp2(sink - m)
            o = (pv[0:A_HEAD_DIM, :] / denom).astype(BF16)
            for hh in range(A_GROUP):
                o_ref[0, g * A_GROUP + hh, :, cols] = o[:, hh * BLK:(hh + 1) * BLK]


def _attn_a(qat, ka, vat, sink_lanes):
    B, _, _, S = qat.shape
    nb = S // BLK
    per = TA // BLK
    prev_i = lambda i: jnp.maximum(i * per - 1, 0)
    next_i = lambda i: jnp.minimum((i + 1) * per, nb - 1)
    k_prev = pl.BlockSpec((1, BLK, A_KV), lambda b, i: (b, prev_i(i), 0))
    k_cur = pl.BlockSpec((1, TA, A_KV), lambda b, i: (b, i, 0))
    k_next = pl.BlockSpec((1, BLK, A_KV), lambda b, i: (b, next_i(i), 0))
    v_prev = pl.BlockSpec((1, A_KV_HEADS, V_EXT, BLK), lambda b, i: (b, 0, 0, prev_i(i)))
    v_cur = pl.BlockSpec((1, A_KV_HEADS, V_EXT, TA), lambda b, i: (b, 0, 0, i))
    v_next = pl.BlockSpec((1, A_KV_HEADS, V_EXT, BLK), lambda b, i: (b, 0, 0, next_i(i)))
    return pl.pallas_call(
        functools.partial(_attn_a_kernel, seq=S),
        grid=(B, S // TA),
        in_specs=[
            _const_spec((A_KV_HEADS, 1, A_GROUP * BLK)),
            pl.BlockSpec((1, A_HEADS, LANES, TA), lambda b, i: (b, 0, 0, i)),
            k_prev, k_cur, k_next, v_prev, v_cur, v_next,
        ],
        out_specs=pl.BlockSpec((1, A_HEADS, A_HEAD_DIM, TA), lambda b, i: (b, 0, 0, i)),
        out_shape=jax.ShapeDtypeStruct((B, A_HEADS, A_HEAD_DIM, S), BF16),
        compiler_params=_params(("parallel", "parallel")),
        name="attn_a",
    )(sink_lanes, qat, ka, ka, ka, vat, vat, vat)


def _mla_kernel(qt_ref, k_ref, vt_ref, o_ref, m_ref, acc_ref, *s_bufs):
    j = pl.program_id(2)
    items = [(c, h) for c in range(TK // TKC) for h in range(B_HEADS)]
    depth = len(s_bufs)
    row0 = pl.multiple_of(jnp.minimum(j, 0), TKC)

    @pl.when(j == 0)
    def _():
        m_ref[...] = jnp.full(m_ref.shape, NEG, F32)
        acc_ref[...] = jnp.zeros(acc_ref.shape, F32)

    def produce(n):
        c, h = items[n]
        s = jnp.dot(k_ref[0, h, c * TKC:(c + 1) * TKC, :], qt_ref[0, h],
                    preferred_element_type=F32)
        s_bufs[n % depth][...] = s
        return jnp.max(s, axis=0, keepdims=True)

    def consume(n, col_max):
        c, h = items[n]
        m_old = m_ref[h]
        m_new = jnp.maximum(m_old, col_max)
        p = jnp.exp2(s_bufs[n % depth][pl.ds(row0, TKC), :] - m_new).astype(BF16)
        pv = jnp.dot(vt_ref[0, c, h], p, preferred_element_type=F32)
        acc_ref[h] = jnp.exp2(m_old - m_new) * acc_ref[h] + pv
        m_ref[h] = m_new

    col_max = {n: produce(n) for n in range(min(depth - 1, len(items)))}
    for n in range(len(items)):
        ahead = n + depth - 1
        if ahead < len(items):
            col_max[ahead] = produce(ahead)
        consume(n, col_max.pop(n))

    @pl.when(j == pl.num_programs(2) - 1)
    def _():
        for h in range(B_HEADS):
            o_ref[0, h] = (acc_ref[h, 0:B_V, :] / acc_ref[h, B_V:B_V + 1, :]).astype(BF16)


def _mla(qt, kcat, vt):
    B, _, _, S = qt.shape
    return pl.pallas_call(
        _mla_kernel,
        grid=(B, S // TQ, S // TK),
        in_specs=[
            pl.BlockSpec((1, B_HEADS, B_SLAB, TQ), lambda b, i, j: (b, 0, 0, i)),
            pl.BlockSpec((1, B_HEADS, TK, B_SLAB), lambda b, i, j: (b, 0, j, 0)),
            pl.BlockSpec((1, TK // TKC, B_HEADS, V_EXT, TKC), lambda b, i, j: (b, j, 0, 0, 0)),
        ],
        out_specs=pl.BlockSpec((1, B_HEADS, B_V, TQ), lambda b, i, j: (b, 0, 0, i)),
        out_shape=jax.ShapeDtypeStruct((B, B_HEADS, B_V, S), BF16),
        scratch_shapes=[
            pltpu.VMEM((B_HEADS, 1, TQ), F32),
            pltpu.VMEM((B_HEADS, V_EXT, TQ), F32),
        ] + [pltpu.VMEM((TKC, TQ), F32)] * MLA_DEPTH,
        compiler_params=_params(("parallel", "parallel", "arbitrary")),
        name="mla",
    )(qt, kcat, vt)


def _memkv_kernel(mem_ref, g_ref, w_ref, k_ref, v_ref):
    mn = _rms(mem_ref[0], g_ref[...]).astype(BF16)
    kv = jnp.dot(mn, w_ref[...], preferred_element_type=F32)
    k_ref[0] = kv[:, :C_Q].astype(BF16)
    v_ref[0] = kv[:, C_Q:].astype(BF16)


def _memkv(mem, g, w):
    B = mem.shape[0]
    out = jax.ShapeDtypeStruct((B, N_MEM, C_Q), BF16)
    spec = pl.BlockSpec((1, N_MEM, C_Q), lambda b: (b, 0, 0))
    return pl.pallas_call(
        _memkv_kernel,
        grid=(B,),
        in_specs=[pl.BlockSpec((1, N_MEM, D_MODEL), lambda b: (b, 0, 0)),
                  _const_spec((1, D_MODEL)), _const_spec((D_MODEL, 2 * C_Q))],
        out_specs=(spec, spec),
        out_shape=(out, out),
        compiler_params=_params(("parallel",)),
        name="memkv",
    )(mem, g, w)


def _merge_kernel(x_ref, oat_ref, obt_ref, qc_ref, mk_ref, mv_ref, g_ref, wgl_ref, wa_ref, wb_ref, wc_ref,
                  wout_ref, o_ref):
    x = x_ref[0]
    tm = x.shape[0]
    u = _rms(x, g_ref[...]).astype(BF16)

    oc = []
    for h in range(C_HEADS):
        sl = slice(h * C_HEAD_DIM, (h + 1) * C_HEAD_DIM)
        s = lax.dot_general(qc_ref[0, :, sl], mk_ref[0, :, sl], NT_DIMS,
                            preferred_element_type=F32) * (C_HEAD_DIM ** -0.5)
        e = jnp.exp(s - jnp.max(s, axis=-1, keepdims=True))
        pv = jnp.dot(e.astype(BF16), mv_ref[0, :, sl], preferred_element_type=F32)
        oc.append((pv / jnp.sum(e, axis=-1, keepdims=True)).astype(BF16))
    oc = jnp.concatenate(oc, axis=1)

    branches = (
        lax.dot_general(oat_ref[0].reshape(A_Q, tm), wa_ref[...], TN_DIMS, preferred_element_type=F32),
        lax.dot_general(obt_ref[0].reshape(B_OUT, tm), wb_ref[...], TN_DIMS, preferred_element_type=F32),
        jnp.dot(oc, wc_ref[...], preferred_element_type=F32),
    )
    merged = jnp.zeros(x.shape, F32)
    for i, br in enumerate(branches):
        gl = jnp.dot(u, wgl_ref[:, i * D_MODEL:(i + 1) * D_MODEL], preferred_element_type=F32)
        merged = merged + br * (1.0 / (1.0 + jnp.exp(-gl)))
    o_ref[0] = x + jnp.dot(merged.astype(BF16), wout_ref[...], preferred_element_type=F32)


def _merge(x1, oat, obt, qc, mk, mv, g, wgl, wa, wb, wc, wout):
    B, S, D = x1.shape
    tok = lambda w: pl.BlockSpec((1, TM_MERGE, w), lambda b, i: (b, i, 0))
    head_t = lambda heads, rows: pl.BlockSpec((1, heads, rows, TM_MERGE), lambda b, i: (b, 0, 0, i))
    mem = pl.BlockSpec((1, N_MEM, C_Q), lambda b, i: (b, 0, 0))
    return pl.pallas_call(
        _merge_kernel,
        grid=(B, S // TM_MERGE),
        in_specs=[
            tok(D), head_t(A_HEADS, A_HEAD_DIM), head_t(B_HEADS, B_V), tok(C_Q), mem, mem,
            _const_spec((1, D)), _const_spec((D, N_BRANCH * D)),
            _const_spec((A_Q, D)), _const_spec((B_OUT, D)), _const_spec((C_Q, D)),
            _const_spec((D, D)),
        ],
        out_specs=tok(D),
        out_shape=jax.ShapeDtypeStruct((B, S, D), F32),
        compiler_params=_params(("parallel", "parallel")),
        name="merge",
    )(x1, oat, obt, qc, mk, mv, g, wgl, wa, wb, wc, wout)


def _rope_tables(seq, half, first_lane, period):
    inv = ROPE_THETA ** (-jnp.arange(0, 2 * half, 2, dtype=F32) / (2 * half))
    ang = jnp.arange(seq, dtype=F32)[:, None] * inv[None, :]
    cos, sin = jnp.cos(ang), jnp.sin(ang)
    lane = jnp.arange(LANES) % period - first_lane
    in_first = (lane >= 0) & (lane < half)
    in_second = (lane >= half) & (lane < 2 * half)
    idx = jnp.clip(jnp.where(in_second, lane - half, lane), 0, half - 1)
    cos_l, sin_l = cos[:, idx], sin[:, idx]
    c = jnp.where((in_first | in_second)[None, :], cos_l, 1.0)
    sa = jnp.where(in_first[None, :], -sin_l, 0.0)
    sb = jnp.where(in_second[None, :], sin_l, 0.0)
    return jnp.stack([c, sa, sb]).astype(F32)


def _prepare(ffn1_norm, ffn1_w_in, ffn1_w_out, mix_norm, w_in, mla_q_norm, mla_w_uq, mla_kv_norm,
             mla_w_ukv, attn_sink, mem_norm, w_mem_kv, w_branch_a, w_branch_b, w_branch_c, w_out,
             ffn2_norm, ffn2_w_in, ffn2_w_out, final_norm):
    row = lambda v: v.reshape(1, -1).astype(F32)
    cols, o = [], 0
    for s in IN_SIZES:
        cols.append(w_in[0][:, o:o + s])
        o += s
    w_qa, w_ka, w_va, w_cq, w_ckv, w_kr, w_qc, w_gl = cols
    w_kr_pad = jnp.zeros((D_MODEL, LANES), F32).at[:, B_NOPE:B_NOPE + B_ROPE].set(w_kr)
    w1 = jnp.concatenate([w_ka, w_cq, w_ckv, w_qc, w_kr_pad], axis=1).astype(BF16)
    wt = jnp.concatenate([w_qa, w_va], axis=1).T.astype(BF16)
    wuq = mla_w_uq[0].reshape(B_Q_LORA, B_HEADS, B_NOPE + B_ROPE)
    wuqt = jnp.pad(wuq, ((0, 0), (0, 0), (0, B_SLAB - B_NOPE - B_ROPE))).reshape(B_Q_LORA, B_CAT).T.astype(BF16)
    wukv = mla_w_ukv[0].reshape(B_KV_LORA, B_HEADS, B_NOPE + B_V)
    wk = jnp.pad(wukv[:, :, :B_NOPE], ((0, 0), (0, 0), (0, B_SLAB - B_NOPE))).reshape(B_KV_LORA, B_CAT).astype(BF16)
    wvt = wukv[:, :, B_NOPE:].reshape(B_KV_LORA, B_OUT).T.astype(BF16)
    sink_lanes = jnp.repeat(attn_sink[0].astype(F32) * LOG2E, BLK).reshape(A_KV_HEADS, 1, A_GROUP * BLK)
    return dict(
        ffn1=(row(ffn1_norm[0]), ffn1_w_in[0].astype(BF16), ffn1_w_out[0].astype(BF16)),
        ffn2=(row(ffn2_norm[0]), ffn2_w_in[0].astype(BF16), ffn2_w_out[0].astype(BF16)),
        final=row(final_norm),
        inproj=(row(mix_norm[0]), w1, wt),
        mla=(row(mla_q_norm[0]), wuqt, row(mla_kv_norm[0]), wk, wvt),
        sink=sink_lanes,
        mem=(row(mem_norm[0]), w_mem_kv[0].astype(BF16)),
        merge=(row(mix_norm[0]), w_gl.astype(BF16), w_branch_a[0].astype(BF16), w_branch_b[0].astype(BF16),
               w_branch_c[0].astype(BF16), w_out[0].astype(BF16)),
    )


def _trunk(x, mem, w):
    S = x.shape[1]
    tab_a = _rope_tables(S, A_HEAD_DIM // 2, 0, A_HEAD_DIM)
    tab_at = jnp.swapaxes(tab_a[:, :, :A_HEAD_DIM], 1, 2)
    tab_b = _rope_tables(S, B_ROPE // 2, B_NOPE, LANES)
    tab_bt = jnp.swapaxes(tab_b, 1, 2)
    x1 = _ffn(x, *w["ffn1"], w["final"], final_norm=False)
    qat, ka, vat, qt, kcat, vt, qc = _inproj(x1, *w["inproj"], tab_a, tab_at, tab_b, tab_bt, *w["mla"])
    oat = _attn_a(qat, ka, vat, w["sink"])
    obt = _mla(qt, kcat, vt)
    mk, mv = _memkv(mem, *w["mem"])
    x2 = _merge(x1, oat, obt, qc, mk, mv, *w["merge"])
    return _ffn(x2, *w["ffn2"], w["final"], final_norm=True)


def kernel(x_prompt, x_sample, mem_prompt, mem_sample, ffn1_norm, ffn1_w_in, ffn1_w_out, mix_norm, w_in,
           mla_q_norm, mla_w_uq, mla_kv_norm, mla_w_ukv, attn_sink, mem_norm, w_mem_kv, w_branch_a,
           w_branch_b, w_branch_c, w_out, ffn2_norm, ffn2_w_in, ffn2_w_out, final_norm):
    w = _prepare(ffn1_norm, ffn1_w_in, ffn1_w_out, mix_norm, w_in, mla_q_norm, mla_w_uq, mla_kv_norm,
                 mla_w_ukv, attn_sink, mem_norm, w_mem_kv, w_branch_a, w_branch_b, w_branch_c, w_out,
                 ffn2_norm, ffn2_w_in, ffn2_w_out, final_norm)
    return (_trunk(x_prompt, mem_prompt, w), _trunk(x_sample, mem_sample, w))
```

```python
import functools

import jax
import jax.numpy as jnp
from jax import lax
from jax.experimental import pallas as pl
from jax.experimental.pallas import tpu as pltpu

D_MODEL = 1024
N_MEM = 256
BLK = 128
WINDOW = 128
ROPE_THETA = 10000.0
EPS = 1e-6
NEG = -1e30
A_HEADS = 8
A_KV_HEADS = 2
A_GROUP = A_HEADS // A_KV_HEADS
A_HEAD_DIM = 64
A_Q = A_HEADS * A_HEAD_DIM
A_KV = A_KV_HEADS * A_HEAD_DIM
B_HEADS = 8
B_Q_LORA = 384
B_KV_LORA = 256
B_NOPE = 64
B_ROPE = 32
B_V = 64
B_OUT = B_HEADS * B_V
C_HEADS = 4
C_HEAD_DIM = 128
C_Q = C_HEADS * C_HEAD_DIM
N_BRANCH = 3
D_FF = 2816
IN_SIZES = (A_Q, A_KV, A_KV, B_Q_LORA, B_KV_LORA, B_ROPE, C_Q, N_BRANCH * D_MODEL)

LANES = 128
B_SLAB = LANES
B_CAT = B_HEADS * B_SLAB
OFF_KA, OFF_CQ, OFF_CKV, OFF_QC, OFF_KR = 0, 128, 512, 768, 1280
W1_COLS = 1408
WT_ROWS = A_Q + A_KV

TM_FFN = 512
FF_CHUNK = 512
TM_IN = 512
TA = 512
TQ = 512
TK = 2048
TKC = 256
PIPE_DEPTH = 3
V_EXT = 80
TM_MERGE = 512
VMEM_LIMIT = 52 * 1024 * 1024
LOG2E = 1.4426950408889634

BF16 = jnp.bfloat16
F32 = jnp.float32
NT_DIMS = (((1,), (1,)), ((), ()))
TN_DIMS = (((0,), (0,)), ((), ()))


def _rms(x, g):
    return x * lax.rsqrt(jnp.mean(x * x, axis=-1, keepdims=True) + EPS) * g


def _const_spec(shape):
    nd = len(shape)
    return pl.BlockSpec(shape, lambda *_: (0,) * nd, pipeline_mode=pl.Buffered(1))


def _params(sem):
    return pltpu.CompilerParams(dimension_semantics=sem, vmem_limit_bytes=VMEM_LIMIT)


def _ones_row_tile(width):
    return (lax.broadcasted_iota(jnp.int32, (V_EXT - B_V, width), 0) == 0).astype(BF16)


def _ffn_kernel(x_ref, g_ref, w1_ref, w2_ref, fn_ref, o_ref, *, final_norm):
    x = x_ref[0]
    xn = _rms(x, g_ref[...]).astype(BF16)
    acc = jnp.zeros(x.shape, F32)
    lo = 0
    while lo < D_FF:
        fc = min(FF_CHUNK, D_FF - lo)
        g = jnp.dot(xn, w1_ref[:, lo:lo + fc], preferred_element_type=F32)
        u = jnp.dot(xn, w1_ref[:, D_FF + lo:D_FF + lo + fc], preferred_element_type=F32)
        h = (g * (1.0 / (1.0 + jnp.exp(-g))) * u).astype(BF16)
        acc = acc + jnp.dot(h, w2_ref[lo:lo + fc, :], preferred_element_type=F32)
        lo += fc
    y = x + 0.5 * acc
    if final_norm:
        y = _rms(y, fn_ref[...])
    o_ref[0] = y


def _ffn(x, g, w1, w2, fn, final_norm):
    B, S, D = x.shape
    return pl.pallas_call(
        functools.partial(_ffn_kernel, final_norm=final_norm),
        grid=(B, S // TM_FFN),
        in_specs=[
            pl.BlockSpec((1, TM_FFN, D), lambda b, i: (b, i, 0)),
            _const_spec((1, D)),
            _const_spec((D, 2 * D_FF)),
            _const_spec((D_FF, D)),
            _const_spec((1, D)),
        ],
        out_specs=pl.BlockSpec((1, TM_FFN, D), lambda b, i: (b, i, 0)),
        out_shape=jax.ShapeDtypeStruct((B, S, D), F32),
        compiler_params=_params(("parallel", "parallel")),
        name="ffn_final" if final_norm else "ffn",
    )(x, g, w1, w2, fn)


def _rope(v, tab_ref, half, width, axis):
    return (v * tab_ref[0]
            + pltpu.roll(v, width - half, axis) * tab_ref[1]
            + pltpu.roll(v, half, axis) * tab_ref[2])


def _inproj_kernel(x_ref, g_ref, w1_ref, wt_ref, ta_ref, tat_ref, tb_ref, tbt_ref, qn_ref, wuqt_ref, kvn_ref,
                   wk_ref, wvt_ref, qat_ref, ka_ref, vat_ref, qt_ref, kcat_ref, vt_ref, qc_ref):
    tm = x_ref.shape[1]
    u = _rms(x_ref[0], g_ref[...]).astype(BF16)
    z = jnp.dot(u, w1_ref[...], preferred_element_type=F32)
    zt = lax.dot_general(wt_ref[...], u, NT_DIMS, preferred_element_type=F32)

    rope_a = functools.partial(_rope, tab_ref=ta_ref, half=A_HEAD_DIM // 2, width=LANES, axis=1)
    rope_at = functools.partial(_rope, tab_ref=tat_ref, half=A_HEAD_DIM // 2, width=A_HEAD_DIM, axis=0)
    rope_b = functools.partial(_rope, tab_ref=tb_ref, half=B_ROPE // 2, width=LANES, axis=1)
    rope_bt = functools.partial(_rope, tab_ref=tbt_ref, half=B_ROPE // 2, width=LANES, axis=0)

    scale_a = A_HEAD_DIM ** -0.5 * LOG2E
    zeros_half = jnp.zeros((A_HEAD_DIM, tm), BF16)
    for h in range(A_HEADS):
        q = (rope_at(zt[h * A_HEAD_DIM:(h + 1) * A_HEAD_DIM, :]) * scale_a).astype(BF16)
        g_kv = h // A_GROUP
        qat_ref[0, h, g_kv * A_HEAD_DIM:(g_kv + 1) * A_HEAD_DIM, :] = q
        qat_ref[0, h, (1 - g_kv) * A_HEAD_DIM:(2 - g_kv) * A_HEAD_DIM, :] = zeros_half
    ka_ref[0] = rope_a(z[:, OFF_KA:OFF_KA + A_KV]).astype(BF16)
    ones_a = _ones_row_tile(tm)
    for g_kv in range(A_KV_HEADS):
        vat_ref[0, g_kv, 0:A_HEAD_DIM, :] = zt[A_Q + g_kv * A_HEAD_DIM:A_Q + (g_kv + 1) * A_HEAD_DIM, :].astype(BF16)
        vat_ref[0, g_kv, A_HEAD_DIM:V_EXT, :] = ones_a

    qc_ref[0] = z[:, OFF_QC:OFF_QC + C_Q].astype(BF16)

    cqn = _rms(z[:, OFF_CQ:OFF_CQ + B_Q_LORA], qn_ref[...]).astype(BF16)
    qt = lax.dot_general(wuqt_ref[...], cqn, NT_DIMS, preferred_element_type=F32)
    scale_b = (B_NOPE + B_ROPE) ** -0.5 * LOG2E
    for h in range(B_HEADS):
        qt_ref[0, h] = (rope_bt(qt[h * B_SLAB:(h + 1) * B_SLAB, :]) * scale_b).astype(BF16)

    cn = _rms(z[:, OFF_CKV:OFF_CKV + B_KV_LORA], kvn_ref[...]).astype(BF16)
    kn = jnp.dot(cn, wk_ref[...], preferred_element_type=F32)
    kr = rope_b(z[:, OFF_KR:OFF_KR + LANES])
    for h in range(B_HEADS):
        kcat_ref[0, h] = (kn[:, h * B_SLAB:(h + 1) * B_SLAB] + kr).astype(BF16)

    vt = lax.dot_general(wvt_ref[...], cn, NT_DIMS, preferred_element_type=F32)
    ones_b = _ones_row_tile(TKC)
    for c in range(tm // TKC):
        for h in range(B_HEADS):
            vt_ref[0, c, h, 0:B_V, :] = vt[h * B_V:(h + 1) * B_V, c * TKC:(c + 1) * TKC].astype(BF16)
            vt_ref[0, c, h, B_V:V_EXT, :] = ones_b


def _inproj(x1, g, w1, wt, tab_a, tab_at, tab_b, tab_bt, qn, wuqt, kvn, wk, wvt):
    B, S, D = x1.shape
    tok = lambda w: pl.BlockSpec((1, TM_IN, w), lambda b, i: (b, i, 0))
    tab = pl.BlockSpec((3, TM_IN, LANES), lambda b, i: (0, i, 0))
    tab_t = lambda rows: pl.BlockSpec((3, rows, TM_IN), lambda b, i: (0, 0, i))
    head_t = lambda heads, rows: pl.BlockSpec((1, heads, rows, TM_IN), lambda b, i: (b, 0, 0, i))
    out_shape = (
        jax.ShapeDtypeStruct((B, A_HEADS, LANES, S), BF16),
        jax.ShapeDtypeStruct((B, S, A_KV), BF16),
        jax.ShapeDtypeStruct((B, A_KV_HEADS, V_EXT, S), BF16),
        jax.ShapeDtypeStruct((B, B_HEADS, B_SLAB, S), BF16),
        jax.ShapeDtypeStruct((B, B_HEADS, S, B_SLAB), BF16),
        jax.ShapeDtypeStruct((B, S // TKC, B_HEADS, V_EXT, TKC), BF16),
        jax.ShapeDtypeStruct((B, S, C_Q), BF16),
    )
    out_specs = (
        head_t(A_HEADS, LANES), tok(A_KV), head_t(A_KV_HEADS, V_EXT), head_t(B_HEADS, B_SLAB),
        pl.BlockSpec((1, B_HEADS, TM_IN, B_SLAB), lambda b, i: (b, 0, i, 0)),
        pl.BlockSpec((1, TM_IN // TKC, B_HEADS, V_EXT, TKC), lambda b, i: (b, i, 0, 0, 0)),
        tok(C_Q),
    )
    return pl.pallas_call(
        _inproj_kernel,
        grid=(B, S // TM_IN),
        in_specs=[
            tok(D), _const_spec((1, D)), _const_spec((D, W1_COLS)), _const_spec((WT_ROWS, D)),
            tab, tab_t(A_HEAD_DIM), tab, tab_t(LANES),
            _const_spec((1, B_Q_LORA)), _const_spec((B_CAT, B_Q_LORA)),
            _const_spec((1, B_KV_LORA)), _const_spec((B_KV_LORA, B_CAT)),
            _const_spec((B_OUT, B_KV_LORA)),
        ],
        out_specs=out_specs,
        out_shape=out_shape,
        compiler_params=_params(("parallel", "parallel")),
        name="inproj",
    )(x1, g, w1, wt, tab_a, tab_at, tab_b, tab_bt, qn, wuqt, kvn, wk, wvt)


def _attn_a_kernel(sink_ref, qt_ref, kp_ref, kc_ref, kn_ref, vp_ref, vc_ref, vn_ref, o_ref, *s_bufs, seq):
    i = pl.program_id(1)
    depth = len(s_bufs)
    row0 = pl.multiple_of(jnp.minimum(i, 0), 3 * BLK)
    k_all = jnp.concatenate([kp_ref[0], kc_ref[0], kn_ref[0]], axis=0)
    v_all = [jnp.concatenate([vp_ref[0, g], vc_ref[0, g], vn_ref[0, g]], axis=1)
             for g in range(A_KV_HEADS)]

    per = TA // BLK
    width = A_GROUP * BLK
    c = lax.broadcasted_iota(jnp.int32, (3 * BLK, width), 0)
    r = lax.broadcasted_iota(jnp.int32, (3 * BLK, width), 1) & (BLK - 1)
    bias_mid = jnp.where(jnp.abs(c - BLK - r) <= WINDOW, 0.0, NEG)
    bias_first = jnp.where(i * TA - BLK + c >= 0, bias_mid, NEG)
    bias_last = jnp.where(i * TA + (per - 2) * BLK + c < seq, bias_mid, NEG)
    biases = [bias_first] + [bias_mid] * (per - 2) + [bias_last]

    items = [(j, g) for j in range(per) for g in range(A_KV_HEADS)]

    def produce(n):
        j, g = items[n]
        cols = slice(j * BLK, (j + 1) * BLK)
        qt = jnp.concatenate([qt_ref[0, g * A_GROUP + hh, :, cols] for hh in range(A_GROUP)], axis=1)
        s = jnp.dot(k_all[j * BLK:(j + 3) * BLK], qt, preferred_element_type=F32) + biases[j]
        s_bufs[n % depth][...] = s
        return jnp.max(s, axis=0, keepdims=True)

    def consume(n, col_max):
        j, g = items[n]
        cols = slice(j * BLK, (j + 1) * BLK)
        sink = sink_ref[g]
        m = jnp.maximum(col_max, sink)
        p = jnp.exp2(s_bufs[n % depth][pl.ds(row0, 3 * BLK), :] - m).astype(BF16)
        pv = jnp.dot(v_all[g][:, j * BLK:(j + 3) * BLK], p, preferred_element_type=F32)
        denom = pv[A_HEAD_DIM:A_HEAD_DIM + 1, :] + jnp.exp2(sink - m)
        o = (pv[0:A_HEAD_DIM, :] / denom).astype(BF16)
        for hh in range(A_GROUP):
            o_ref[0, g * A_GROUP + hh, :, cols] = o[:, hh * BLK:(hh + 1) * BLK]

    col_max = {n: produce(n) for n in range(min(depth - 1, len(items)))}
    for n in range(len(items)):
        ahead = n + depth - 1
        if ahead < len(items):
            col_max[ahead] = produce(ahead)
        consume(n, col_max.pop(n))


def _attn_a(qat, ka, vat, sink_lanes):
    B, _, _, S = qat.shape
    nb = S // BLK
    per = TA // BLK
    prev_i = lambda i: jnp.maximum(i * per - 1, 0)
    next_i = lambda i: jnp.minimum((i + 1) * per, nb - 1)
    k_prev = pl.BlockSpec((1, BLK, A_KV), lambda b, i: (b, prev_i(i), 0))
    k_cur = pl.BlockSpec((1, TA, A_KV), lambda b, i: (b, i, 0))
    k_next = pl.BlockSpec((1, BLK, A_KV), lambda b, i: (b, next_i(i), 0))
    v_prev = pl.BlockSpec((1, A_KV_HEADS, V_EXT, BLK), lambda b, i: (b, 0, 0, prev_i(i)))
    v_cur = pl.BlockSpec((1, A_KV_HEADS, V_EXT, TA), lambda b, i: (b, 0, 0, i))
    v_next = pl.BlockSpec((1, A_KV_HEADS, V_EXT, BLK), lambda b, i: (b, 0, 0, next_i(i)))
    return pl.pallas_call(
        functools.partial(_attn_a_kernel, seq=S),
        grid=(B, S // TA),
        in_specs=[
            _const_spec((A_KV_HEADS, 1, A_GROUP * BLK)),
            pl.BlockSpec((1, A_HEADS, LANES, TA), lambda b, i: (b, 0, 0, i)),
            k_prev, k_cur, k_next, v_prev, v_cur, v_next,
        ],
        out_specs=pl.BlockSpec((1, A_HEADS, A_HEAD_DIM, TA), lambda b, i: (b, 0, 0, i)),
        out_shape=jax.ShapeDtypeStruct((B, A_HEADS, A_HEAD_DIM, S), BF16),
        scratch_shapes=[pltpu.VMEM((3 * BLK, A_GROUP * BLK), F32)] * PIPE_DEPTH,
        compiler_params=_params(("parallel", "parallel")),
        name="attn_a",
    )(sink_lanes, qat, ka, ka, ka, vat, vat, vat)


def _mla_kernel(qt_ref, k_ref, vt_ref, o_ref, m_ref, acc_ref, *s_bufs):
    j = pl.program_id(2)
    items = [(c, h) for c in range(TK // TKC) for h in range(B_HEADS)]
    depth = len(s_bufs)
    row0 = pl.multiple_of(jnp.minimum(j, 0), TKC)

    @pl.when(j == 0)
    def _():
        m_ref[...] = jnp.full(m_ref.shape, NEG, F32)
        acc_ref[...] = jnp.zeros(acc_ref.shape, F32)

    def produce(n):
        c, h = items[n]
        s = jnp.dot(k_ref[0, h, c * TKC:(c + 1) * TKC, :], qt_ref[0, h],
                    preferred_element_type=F32)
        s_bufs[n % depth][...] = s
        return jnp.max(s, axis=0, keepdims=True)

    def consume(n, col_max):
        c, h = items[n]
        m_old = m_ref[h]
        m_new = jnp.maximum(m_old, col_max)
        p = jnp.exp2(s_bufs[n % depth][pl.ds(row0, TKC), :] - m_new).astype(BF16)
        pv = jnp.dot(vt_ref[0, c, h], p, preferred_element_type=F32)
        acc_ref[h] = jnp.exp2(m_old - m_new) * acc_ref[h] + pv
        m_ref[h] = m_new

    col_max = {n: produce(n) for n in range(min(depth - 1, len(items)))}
    for n in range(len(items)):
        ahead = n + depth - 1
        if ahead < len(items):
            col_max[ahead] = produce(ahead)
        consume(n, col_max.pop(n))

    @pl.when(j == pl.num_programs(2) - 1)
    def _():
        for h in range(B_HEADS):
            o_ref[0, h] = (acc_ref[h, 0:B_V, :] / acc_ref[h, B_V:B_V + 1, :]).astype(BF16)


def _mla(qt, kcat, vt):
    B, _, _, S = qt.shape
    return pl.pallas_call(
        _mla_kernel,
        grid=(B, S // TQ, S // TK),
        in_specs=[
            pl.BlockSpec((1, B_HEADS, B_SLAB, TQ), lambda b, i, j: (b, 0, 0, i)),
            pl.BlockSpec((1, B_HEADS, TK, B_SLAB), lambda b, i, j: (b, 0, j, 0)),
            pl.BlockSpec((1, TK // TKC, B_HEADS, V_EXT, TKC), lambda b, i, j: (b, j, 0, 0, 0)),
        ],
        out_specs=pl.BlockSpec((1, B_HEADS, B_V, TQ), lambda b, i, j: (b, 0, 0, i)),
        out_shape=jax.ShapeDtypeStruct((B, B_HEADS, B_V, S), BF16),
        scratch_shapes=[
            pltpu.VMEM((B_HEADS, 1, TQ), F32),
            pltpu.VMEM((B_HEADS, V_EXT, TQ), F32),
        ] + [pltpu.VMEM((TKC, TQ), F32)] * PIPE_DEPTH,
        compiler_params=_params(("parallel", "parallel", "arbitrary")),
        name="mla",
    )(qt, kcat, vt)


def _memkv_kernel(mem_ref, g_ref, w_ref, k_ref, v_ref):
    mn = _rms(mem_ref[0], g_ref[...]).astype(BF16)
    kv = jnp.dot(mn, w_ref[...], preferred_element_type=F32)
    k_ref[0] = kv[:, :C_Q].astype(BF16)
    v_ref[0] = kv[:, C_Q:].astype(BF16)


def _memkv(mem, g, w):
    B = mem.shape[0]
    out = jax.ShapeDtypeStruct((B, N_MEM, C_Q), BF16)
    spec = pl.BlockSpec((1, N_MEM, C_Q), lambda b: (b, 0, 0))
    return pl.pallas_call(
        _memkv_kernel,
        grid=(B,),
        in_specs=[pl.BlockSpec((1, N_MEM, D_MODEL), lambda b: (b, 0, 0)),
                  _const_spec((1, D_MODEL)), _const_spec((D_MODEL, 2 * C_Q))],
        out_specs=(spec, spec),
        out_shape=(out, out),
        compiler_params=_params(("parallel",)),
        name="memkv",
    )(mem, g, w)


def _merge_kernel(x_ref, oat_ref, obt_ref, qc_ref, mk_ref, mv_ref, g_ref, wgl_ref, wa_ref, wb_ref, wc_ref,
                  wout_ref, o_ref):
    x = x_ref[0]
    tm = x.shape[0]
    u = _rms(x, g_ref[...]).astype(BF16)

    oc = []
    for h in range(C_HEADS):
        sl = slice(h * C_HEAD_DIM, (h + 1) * C_HEAD_DIM)
        s = lax.dot_general(qc_ref[0, :, sl], mk_ref[0, :, sl], NT_DIMS,
                            preferred_element_type=F32) * (C_HEAD_DIM ** -0.5)
        e = jnp.exp(s - jnp.max(s, axis=-1, keepdims=True))
        pv = jnp.dot(e.astype(BF16), mv_ref[0, :, sl], preferred_element_type=F32)
        oc.append((pv / jnp.sum(e, axis=-1, keepdims=True)).astype(BF16))
    oc = jnp.concatenate(oc, axis=1)

    branches = (
        lax.dot_general(oat_ref[0].reshape(A_Q, tm), wa_ref[...], TN_DIMS, preferred_element_type=F32),
        lax.dot_general(obt_ref[0].reshape(B_OUT, tm), wb_ref[...], TN_DIMS, preferred_element_type=F32),
        jnp.dot(oc, wc_ref[...], preferred_element_type=F32),
    )
    merged = jnp.zeros(x.shape, F32)
    for i, br in enumerate(branches):
        gl = jnp.dot(u, wgl_ref[:, i * D_MODEL:(i + 1) * D_MODEL], preferred_element_type=F32)
        merged = merged + br * (1.0 / (1.0 + jnp.exp(-gl)))
    o_ref[0] = x + jnp.dot(merged.astype(BF16), wout_ref[...], preferred_element_type=F32)


def _merge(x1, oat, obt, qc, mk, mv, g, wgl, wa, wb, wc, wout):
    B, S, D = x1.shape
    tok = lambda w: pl.BlockSpec((1, TM_MERGE, w), lambda b, i: (b, i, 0))
    head_t = lambda heads, rows: pl.BlockSpec((1, heads, rows, TM_MERGE), lambda b, i: (b, 0, 0, i))
    mem = pl.BlockSpec((1, N_MEM, C_Q), lambda b, i: (b, 0, 0))
    return pl.pallas_call(
        _merge_kernel,
        grid=(B, S // TM_MERGE),
        in_specs=[
            tok(D), head_t(A_HEADS, A_HEAD_DIM), head_t(B_HEADS, B_V), tok(C_Q), mem, mem,
            _const_spec((1, D)), _const_spec((D, N_BRANCH * D)),
            _const_spec((A_Q, D)), _const_spec((B_OUT, D)), _const_spec((C_Q, D)),
            _const_spec((D, D)),
        ],
        out_specs=tok(D),
        out_shape=jax.ShapeDtypeStruct((B, S, D), F32),
        compiler_params=_params(("parallel", "parallel")),
        name="merge",
    )(x1, oat, obt, qc, mk, mv, g, wgl, wa, wb, wc, wout)


def _rope_tables(seq, half, pad_before, pad_after, repeat, transposed):
    inv = ROPE_THETA ** (-jnp.arange(0, 2 * half, 2, dtype=F32) / (2 * half))
    pos = jnp.arange(seq, dtype=F32)
    ang = inv[:, None] * pos[None, :] if transposed else pos[:, None] * inv[None, :]
    cos, sin = jnp.cos(ang), jnp.sin(ang)
    axis = 0 if transposed else 1

    def fill(value, n):
        shape = (n, seq) if transposed else (seq, n)
        return [jnp.full(shape, value, F32)] if n else []

    def table(first, second, pad_value):
        unit = fill(pad_value, pad_before) + [first, second] + fill(pad_value, pad_after)
        return jnp.concatenate(unit * repeat, axis=axis)

    zero = jnp.zeros_like(sin)
    return jnp.stack([table(cos, cos, 1.0), table(-sin, zero, 0.0), table(zero, sin, 0.0)])


def _prepare(ffn1_norm, ffn1_w_in, ffn1_w_out, mix_norm, w_in, mla_q_norm, mla_w_uq, mla_kv_norm,
             mla_w_ukv, attn_sink, mem_norm, w_mem_kv, w_branch_a, w_branch_b, w_branch_c, w_out,
             ffn2_norm, ffn2_w_in, ffn2_w_out, final_norm):
    row = lambda v: v.reshape(1, -1).astype(F32)
    cols, o = [], 0
    for s in IN_SIZES:
        cols.append(w_in[0][:, o:o + s])
        o += s
    w_qa, w_ka, w_va, w_cq, w_ckv, w_kr, w_qc, w_gl = cols
    w_kr_pad = jnp.zeros((D_MODEL, LANES), F32).at[:, B_NOPE:B_NOPE + B_ROPE].set(w_kr)
    w1 = jnp.concatenate([w_ka, w_cq, w_ckv, w_qc, w_kr_pad], axis=1).astype(BF16)
    wt = jnp.concatenate([w_qa, w_va], axis=1).T.astype(BF16)
    wuq = mla_w_uq[0].reshape(B_Q_LORA, B_HEADS, B_NOPE + B_ROPE)
    wuqt = jnp.pad(wuq, ((0, 0), (0, 0), (0, B_SLAB - B_NOPE - B_ROPE))).reshape(B_Q_LORA, B_CAT).T.astype(BF16)
    wukv = mla_w_ukv[0].reshape(B_KV_LORA, B_HEADS, B_NOPE + B_V)
    wk = jnp.pad(wukv[:, :, :B_NOPE], ((0, 0), (0, 0), (0, B_SLAB - B_NOPE))).reshape(B_KV_LORA, B_CAT).astype(BF16)
    wvt = wukv[:, :, B_NOPE:].reshape(B_KV_LORA, B_OUT).T.astype(BF16)
    sink_lanes = jnp.repeat(attn_sink[0].astype(F32) * LOG2E, BLK).reshape(A_KV_HEADS, 1, A_GROUP * BLK)
    return dict(
        ffn1=(row(ffn1_norm[0]), ffn1_w_in[0].astype(BF16), ffn1_w_out[0].astype(BF16)),
        ffn2=(row(ffn2_norm[0]), ffn2_w_in[0].astype(BF16), ffn2_w_out[0].astype(BF16)),
        final=row(final_norm),
        inproj=(row(mix_norm[0]), w1, wt),
        mla=(row(mla_q_norm[0]), wuqt, row(mla_kv_norm[0]), wk, wvt),
        sink=sink_lanes,
        mem=(row(mem_norm[0]), w_mem_kv[0].astype(BF16)),
        merge=(row(mix_norm[0]), w_gl.astype(BF16), w_branch_a[0].astype(BF16), w_branch_b[0].astype(BF16),
               w_branch_c[0].astype(BF16), w_out[0].astype(BF16)),
    )


def _all_rope_tables(seq):
    a_half, b_half = A_HEAD_DIM // 2, B_ROPE // 2
    b_pad = (B_NOPE, B_SLAB - B_NOPE - B_ROPE)
    return (_rope_tables(seq, a_half, 0, 0, LANES // A_HEAD_DIM, False),
            _rope_tables(seq, a_half, 0, 0, 1, True),
            _rope_tables(seq, b_half, *b_pad, 1, False),
            _rope_tables(seq, b_half, *b_pad, 1, True))


def _trunk(x, mem, w, tables):
    x1 = _ffn(x, *w["ffn1"], w["final"], final_norm=False)
    qat, ka, vat, qt, kcat, vt, qc = _inproj(x1, *w["inproj"], *tables, *w["mla"])
    oat = _attn_a(qat, ka, vat, w["sink"])
    obt = _mla(qt, kcat, vt)
    mk, mv = _memkv(mem, *w["mem"])
    x2 = _merge(x1, oat, obt, qc, mk, mv, *w["merge"])
    return _ffn(x2, *w["ffn2"], w["final"], final_norm=True)


def kernel(x_prompt, x_sample, mem_prompt, mem_sample, ffn1_norm, ffn1_w_in, ffn1_w_out, mix_norm, w_in,
           mla_q_norm, mla_w_uq, mla_kv_norm, mla_w_ukv, attn_sink, mem_norm, w_mem_kv, w_branch_a,
           w_branch_b, w_branch_c, w_out, ffn2_norm, ffn2_w_in, ffn2_w_out, final_norm):
    w = _prepare(ffn1_norm, ffn1_w_in, ffn1_w_out, mix_norm, w_in, mla_q_norm, mla_w_uq, mla_kv_norm,
                 mla_w_ukv, attn_sink, mem_norm, w_mem_kv, w_branch_a, w_branch_b, w_branch_c, w_out,
                 ffn2_norm, ffn2_w_in, ffn2_w_out, final_norm)
    tables = _all_rope_tables(max(x_prompt.shape[1], x_sample.shape[1]))
    return (_trunk(x_prompt, mem_prompt, w, tables), _trunk(x_sample, mem_sample, w, tables))
```

```python
import functools

import jax
import jax.numpy as jnp
from jax import lax
from jax.experimental import pallas as pl
from jax.experimental.pallas import tpu as pltpu

D_MODEL = 1024
N_MEM = 256
BLK = 128
WINDOW = 128
ROPE_THETA = 10000.0
EPS = 1e-6
NEG = -1e30
A_HEADS = 8
A_KV_HEADS = 2
A_GROUP = A_HEADS // A_KV_HEADS
A_HEAD_DIM = 64
A_Q = A_HEADS * A_HEAD_DIM
A_KV = A_KV_HEADS * A_HEAD_DIM
B_HEADS = 8
B_Q_LORA = 384
B_KV_LORA = 256
B_NOPE = 64
B_ROPE = 32
B_V = 64
B_OUT = B_HEADS * B_V
C_HEADS = 4
C_HEAD_DIM = 128
C_Q = C_HEADS * C_HEAD_DIM
N_BRANCH = 3
D_FF = 2816
IN_SIZES = (A_Q, A_KV, A_KV, B_Q_LORA, B_KV_LORA, B_ROPE, C_Q, N_BRANCH * D_MODEL)

LANES = 128
B_SLAB = LANES
B_CAT = B_HEADS * B_SLAB
OFF_CQ, OFF_CKV, OFF_QC = 0, 384, 640
W1_COLS = 1152
ROW_QA, ROW_VA, ROW_KA, ROW_KR = 0, 512, 640, 768
WT_ROWS = 800

TM_FFN = 512
FF_CHUNK = 512
TM_IN = 512
TA = 512
TQ = 512
TK = 2048
TKC = 256
PIPE_DEPTH = 3
V_EXT = 80
TM_MERGE = 512
VMEM_LIMIT = 52 * 1024 * 1024
LOG2E = 1.4426950408889634

BF16 = jnp.bfloat16
F32 = jnp.float32
NT_DIMS = (((1,), (1,)), ((), ()))
TN_DIMS = (((0,), (0,)), ((), ()))


def _rms(x, g):
    return x * lax.rsqrt(jnp.mean(x * x, axis=-1, keepdims=True) + EPS) * g


def _const_spec(shape):
    nd = len(shape)
    return pl.BlockSpec(shape, lambda *_: (0,) * nd, pipeline_mode=pl.Buffered(1))


def _params(sem):
    return pltpu.CompilerParams(dimension_semantics=sem, vmem_limit_bytes=VMEM_LIMIT)


def _ones_row_tile(width):
    return (lax.broadcasted_iota(jnp.int32, (V_EXT - B_V, width), 0) == 0).astype(BF16)


def _ffn_kernel(x_ref, g_ref, w1_ref, w2_ref, fn_ref, o_ref, *, final_norm):
    x = x_ref[0]
    xn = _rms(x, g_ref[...]).astype(BF16)
    acc = jnp.zeros(x.shape, F32)
    lo = 0
    while lo < D_FF:
        fc = min(FF_CHUNK, D_FF - lo)
        g = jnp.dot(xn, w1_ref[:, lo:lo + fc], preferred_element_type=F32)
        u = jnp.dot(xn, w1_ref[:, D_FF + lo:D_FF + lo + fc], preferred_element_type=F32)
        h = (g * (1.0 / (1.0 + jnp.exp(-g))) * u).astype(BF16)
        acc = acc + jnp.dot(h, w2_ref[lo:lo + fc, :], preferred_element_type=F32)
        lo += fc
    y = x + 0.5 * acc
    if final_norm:
        y = _rms(y, fn_ref[...])
    o_ref[0] = y


def _ffn(x, g, w1, w2, fn, final_norm):
    B, S, D = x.shape
    return pl.pallas_call(
        functools.partial(_ffn_kernel, final_norm=final_norm),
        grid=(B, S // TM_FFN),
        in_specs=[
            pl.BlockSpec((1, TM_FFN, D), lambda b, i: (b, i, 0)),
            _const_spec((1, D)),
            _const_spec((D, 2 * D_FF)),
            _const_spec((D_FF, D)),
            _const_spec((1, D)),
        ],
        out_specs=pl.BlockSpec((1, TM_FFN, D), lambda b, i: (b, i, 0)),
        out_shape=jax.ShapeDtypeStruct((B, S, D), F32),
        compiler_params=_params(("parallel", "parallel")),
        name="ffn_final" if final_norm else "ffn",
    )(x, g, w1, w2, fn)


def _rope_rows(x, tab_ref):
    half = x.shape[0] // 2
    x1, x2 = x[:half], x[half:]
    cos, sin = tab_ref[0], tab_ref[1]
    return jnp.concatenate([x1 * cos - x2 * sin, x2 * cos + x1 * sin], axis=0)


def _inproj_kernel(x_ref, g_ref, w1_ref, wt_ref, ta_ref, tb_ref, qn_ref, wuqt_ref, kvn_ref,
                   wk_ref, wvt_ref, qat_ref, ka_ref, vat_ref, qt_ref, kcat_ref, vt_ref, qc_ref):
    tm = x_ref.shape[1]
    u = _rms(x_ref[0], g_ref[...]).astype(BF16)
    z = jnp.dot(u, w1_ref[...], preferred_element_type=F32)
    zt = lax.dot_general(wt_ref[...], u, NT_DIMS, preferred_element_type=F32)

    scale_a = A_HEAD_DIM ** -0.5 * LOG2E
    zeros_half = jnp.zeros((A_HEAD_DIM, tm), BF16)
    for h in range(A_HEADS):
        rows = slice(ROW_QA + h * A_HEAD_DIM, ROW_QA + (h + 1) * A_HEAD_DIM)
        q = (_rope_rows(zt[rows, :], ta_ref) * scale_a).astype(BF16)
        g_kv = h // A_GROUP
        qat_ref[0, h, g_kv * A_HEAD_DIM:(g_kv + 1) * A_HEAD_DIM, :] = q
        qat_ref[0, h, (1 - g_kv) * A_HEAD_DIM:(2 - g_kv) * A_HEAD_DIM, :] = zeros_half
    ka_t = jnp.concatenate(
        [_rope_rows(zt[ROW_KA + g * A_HEAD_DIM:ROW_KA + (g + 1) * A_HEAD_DIM, :], ta_ref)
         for g in range(A_KV_HEADS)], axis=0)
    ka_ref[0] = ka_t.T.astype(BF16)
    ones_a = _ones_row_tile(tm)
    for g_kv in range(A_KV_HEADS):
        rows = slice(ROW_VA + g_kv * A_HEAD_DIM, ROW_VA + (g_kv + 1) * A_HEAD_DIM)
        vat_ref[0, g_kv, 0:A_HEAD_DIM, :] = zt[rows, :].astype(BF16)
        vat_ref[0, g_kv, A_HEAD_DIM:V_EXT, :] = ones_a

    qc_ref[0] = z[:, OFF_QC:OFF_QC + C_Q].astype(BF16)

    cqn = _rms(z[:, OFF_CQ:OFF_CQ + B_Q_LORA], qn_ref[...]).astype(BF16)
    qt = lax.dot_general(wuqt_ref[...], cqn, NT_DIMS, preferred_element_type=F32)
    scale_b = (B_NOPE + B_ROPE) ** -0.5 * LOG2E
    rope_lo, rope_hi = B_NOPE, B_NOPE + B_ROPE
    for h in range(B_HEADS):
        slab = qt[h * B_SLAB:(h + 1) * B_SLAB, :]
        slab = jnp.concatenate([slab[:rope_lo], _rope_rows(slab[rope_lo:rope_hi], tb_ref), slab[rope_hi:]], axis=0)
        qt_ref[0, h] = (slab * scale_b).astype(BF16)

    cn = _rms(z[:, OFF_CKV:OFF_CKV + B_KV_LORA], kvn_ref[...]).astype(BF16)
    kn = jnp.dot(cn, wk_ref[...], preferred_element_type=F32)
    kr_t = jnp.concatenate([jnp.zeros((rope_lo, tm), F32),
                            _rope_rows(zt[ROW_KR:ROW_KR + B_ROPE, :], tb_ref),
                            jnp.zeros((B_SLAB - rope_hi, tm), F32)], axis=0)
    kr = kr_t.T
    for h in range(B_HEADS):
        kcat_ref[0, h] = (kn[:, h * B_SLAB:(h + 1) * B_SLAB] + kr).astype(BF16)

    vt = lax.dot_general(wvt_ref[...], cn, NT_DIMS, preferred_element_type=F32)
    ones_b = _ones_row_tile(TKC)
    for c in range(tm // TKC):
        for h in range(B_HEADS):
            vt_ref[0, c, h, 0:B_V, :] = vt[h * B_V:(h + 1) * B_V, c * TKC:(c + 1) * TKC].astype(BF16)
            vt_ref[0, c, h, B_V:V_EXT, :] = ones_b


def _inproj(x1, g, w1, wt, tab_a, tab_b, qn, wuqt, kvn, wk, wvt):
    B, S, D = x1.shape
    tok = lambda w: pl.BlockSpec((1, TM_IN, w), lambda b, i: (b, i, 0))
    tab = lambda half: pl.BlockSpec((2, half, TM_IN), lambda b, i: (0, 0, i))
    head_t = lambda heads, rows: pl.BlockSpec((1, heads, rows, TM_IN), lambda b, i: (b, 0, 0, i))
    out_shape = (
        jax.ShapeDtypeStruct((B, A_HEADS, LANES, S), BF16),
        jax.ShapeDtypeStruct((B, S, A_KV), BF16),
        jax.ShapeDtypeStruct((B, A_KV_HEADS, V_EXT, S), BF16),
        jax.ShapeDtypeStruct((B, B_HEADS, B_SLAB, S), BF16),
        jax.ShapeDtypeStruct((B, B_HEADS, S, B_SLAB), BF16),
        jax.ShapeDtypeStruct((B, S // TKC, B_HEADS, V_EXT, TKC), BF16),
        jax.ShapeDtypeStruct((B, S, C_Q), BF16),
    )
    out_specs = (
        head_t(A_HEADS, LANES), tok(A_KV), head_t(A_KV_HEADS, V_EXT), head_t(B_HEADS, B_SLAB),
        pl.BlockSpec((1, B_HEADS, TM_IN, B_SLAB), lambda b, i: (b, 0, i, 0)),
        pl.BlockSpec((1, TM_IN // TKC, B_HEADS, V_EXT, TKC), lambda b, i: (b, i, 0, 0, 0)),
        tok(C_Q),
    )
    return pl.pallas_call(
        _inproj_kernel,
        grid=(B, S // TM_IN),
        in_specs=[
            tok(D), _const_spec((1, D)), _const_spec((D, W1_COLS)), _const_spec((WT_ROWS, D)),
            tab(A_HEAD_DIM // 2), tab(B_ROPE // 2),
            _const_spec((1, B_Q_LORA)), _const_spec((B_CAT, B_Q_LORA)),
            _const_spec((1, B_KV_LORA)), _const_spec((B_KV_LORA, B_CAT)),
            _const_spec((B_OUT, B_KV_LORA)),
        ],
        out_specs=out_specs,
        out_shape=out_shape,
        compiler_params=_params(("parallel", "parallel")),
        name="inproj",
    )(x1, g, w1, wt, tab_a, tab_b, qn, wuqt, kvn, wk, wvt)


def _attn_a_kernel(sink_ref, qt_ref, kp_ref, kc_ref, kn_ref, vp_ref, vc_ref, vn_ref, o_ref, *s_bufs, seq):
    i = pl.program_id(1)
    depth = len(s_bufs)
    row0 = pl.multiple_of(jnp.minimum(i, 0), 3 * BLK)
    k_all = jnp.concatenate([kp_ref[0], kc_ref[0], kn_ref[0]], axis=0)
    v_all = [jnp.concatenate([vp_ref[0, g], vc_ref[0, g], vn_ref[0, g]], axis=1)
             for g in range(A_KV_HEADS)]

    per = TA // BLK
    width = A_GROUP * BLK
    c = lax.broadcasted_iota(jnp.int32, (3 * BLK, width), 0)
    r = lax.broadcasted_iota(jnp.int32, (3 * BLK, width), 1) & (BLK - 1)
    bias_mid = jnp.where(jnp.abs(c - BLK - r) <= WINDOW, 0.0, NEG)
    bias_first = jnp.where(i * TA - BLK + c >= 0, bias_mid, NEG)
    bias_last = jnp.where(i * TA + (per - 2) * BLK + c < seq, bias_mid, NEG)
    biases = [bias_first] + [bias_mid] * (per - 2) + [bias_last]

    items = [(j, g) for j in range(per) for g in range(A_KV_HEADS)]

    def produce(n):
        j, g = items[n]
        cols = slice(j * BLK, (j + 1) * BLK)
        qt = jnp.concatenate([qt_ref[0, g * A_GROUP + hh, :, cols] for hh in range(A_GROUP)], axis=1)
        s = jnp.dot(k_all[j * BLK:(j + 3) * BLK], qt, preferred_element_type=F32) + biases[j]
        s_bufs[n % depth][...] = s
        return jnp.max(s, axis=0, keepdims=True)

    def consume(n, col_max):
        j, g = items[n]
        cols = slice(j * BLK, (j + 1) * BLK)
        sink = sink_ref[g]
        m = jnp.maximum(col_max, sink)
        p = jnp.exp2(s_bufs[n % depth][pl.ds(row0, 3 * BLK), :] - m).astype(BF16)
        pv = jnp.dot(v_all[g][:, j * BLK:(j + 3) * BLK], p, preferred_element_type=F32)
        denom = pv[A_HEAD_DIM:A_HEAD_DIM + 1, :] + jnp.exp2(sink - m)
        o = (pv[0:A_HEAD_DIM, :] / denom).astype(BF16)
        for hh in range(A_GROUP):
            o_ref[0, g * A_GROUP + hh, :, cols] = o[:, hh * BLK:(hh + 1) * BLK]

    col_max = {n: produce(n) for n in range(min(depth - 1, len(items)))}
    for n in range(len(items)):
        ahead = n + depth - 1
        if ahead < len(items):
            col_max[ahead] = produce(ahead)
        consume(n, col_max.pop(n))


def _attn_a(qat, ka, vat, sink_lanes):
    B, _, _, S = qat.shape
    nb = S // BLK
    per = TA // BLK
    prev_i = lambda i: jnp.maximum(i * per - 1, 0)
    next_i = lambda i: jnp.minimum((i + 1) * per, nb - 1)
    k_prev = pl.BlockSpec((1, BLK, A_KV), lambda b, i: (b, prev_i(i), 0))
    k_cur = pl.BlockSpec((1, TA, A_KV), lambda b, i: (b, i, 0))
    k_next = pl.BlockSpec((1, BLK, A_KV), lambda b, i: (b, next_i(i), 0))
    v_prev = pl.BlockSpec((1, A_KV_HEADS, V_EXT, BLK), lambda b, i: (b, 0, 0, prev_i(i)))
    v_cur = pl.BlockSpec((1, A_KV_HEADS, V_EXT, TA), lambda b, i: (b, 0, 0, i))
    v_next = pl.BlockSpec((1, A_KV_HEADS, V_EXT, BLK), lambda b, i: (b, 0, 0, next_i(i)))
    return pl.pallas_call(
        functools.partial(_attn_a_kernel, seq=S),
        grid=(B, S // TA),
        in_specs=[
            _const_spec((A_KV_HEADS, 1, A_GROUP * BLK)),
            pl.BlockSpec((1, A_HEADS, LANES, TA), lambda b, i: (b, 0, 0, i)),
            k_prev, k_cur, k_next, v_prev, v_cur, v_next,
        ],
        out_specs=pl.BlockSpec((1, A_HEADS, A_HEAD_DIM, TA), lambda b, i: (b, 0, 0, i)),
        out_shape=jax.ShapeDtypeStruct((B, A_HEADS, A_HEAD_DIM, S), BF16),
        scratch_shapes=[pltpu.VMEM((3 * BLK, A_GROUP * BLK), F32)] * PIPE_DEPTH,
        compiler_params=_params(("parallel", "parallel")),
        name="attn_a",
    )(sink_lanes, qat, ka, ka, ka, vat, vat, vat)


def _mla_kernel(qt_ref, k_ref, vt_ref, o_ref, m_ref, acc_ref, *s_bufs):
    j = pl.program_id(2)
    items = [(c, h) for c in range(TK // TKC) for h in range(B_HEADS)]
    depth = len(s_bufs)
    row0 = pl.multiple_of(jnp.minimum(j, 0), TKC)

    @pl.when(j == 0)
    def _():
        m_ref[...] = jnp.full(m_ref.shape, NEG, F32)
        acc_ref[...] = jnp.zeros(acc_ref.shape, F32)

    def produce(n):
        c, h = items[n]
        s = jnp.dot(k_ref[0, h, c * TKC:(c + 1) * TKC, :], qt_ref[0, h],
                    preferred_element_type=F32)
        s_bufs[n % depth][...] = s
        return jnp.max(s, axis=0, keepdims=True)

    def consume(n, col_max):
        c, h = items[n]
        m_old = m_ref[h]
        m_new = jnp.maximum(m_old, col_max)
        p = jnp.exp2(s_bufs[n % depth][pl.ds(row0, TKC), :] - m_new).astype(BF16)
        pv = jnp.dot(vt_ref[0, c, h], p, preferred_element_type=F32)
        acc_ref[h] = jnp.exp2(m_old - m_new) * acc_ref[h] + pv
        m_ref[h] = m_new

    col_max = {n: produce(n) for n in range(min(depth - 1, len(items)))}
    for n in range(len(items)):
        ahead = n + depth - 1
        if ahead < len(items):
            col_max[ahead] = produce(ahead)
        consume(n, col_max.pop(n))

    @pl.when(j == pl.num_programs(2) - 1)
    def _():
        for h in range(B_HEADS):
            o_ref[0, h] = (acc_ref[h, 0:B_V, :] / acc_ref[h, B_V:B_V + 1, :]).astype(BF16)


def _mla(qt, kcat, vt):
    B, _, _, S = qt.shape
    return pl.pallas_call(
        _mla_kernel,
        grid=(B, S // TQ, S // TK),
        in_specs=[
            pl.BlockSpec((1, B_HEADS, B_SLAB, TQ), lambda b, i, j: (b, 0, 0, i)),
            pl.BlockSpec((1, B_HEADS, TK, B_SLAB), lambda b, i, j: (b, 0, j, 0)),
            pl.BlockSpec((1, TK // TKC, B_HEADS, V_EXT, TKC), lambda b, i, j: (b, j, 0, 0, 0)),
        ],
        out_specs=pl.BlockSpec((1, B_HEADS, B_V, TQ), lambda b, i, j: (b, 0, 0, i)),
        out_shape=jax.ShapeDtypeStruct((B, B_HEADS, B_V, S), BF16),
        scratch_shapes=[
            pltpu.VMEM((B_HEADS, 1, TQ), F32),
            pltpu.VMEM((B_HEADS, V_EXT, TQ), F32),
        ] + [pltpu.VMEM((TKC, TQ), F32)] * PIPE_DEPTH,
        compiler_params=_params(("parallel", "parallel", "arbitrary")),
        name="mla",
    )(qt, kcat, vt)


def _memkv_kernel(mem_ref, g_ref, w_ref, k_ref, v_ref):
    mn = _rms(mem_ref[0], g_ref[...]).astype(BF16)
    kv = jnp.dot(mn, w_ref[...], preferred_element_type=F32)
    k_ref[0] = kv[:, :C_Q].astype(BF16)
    v_ref[0] = kv[:, C_Q:].astype(BF16)


def _memkv(mem, g, w):
    B = mem.shape[0]
    out = jax.ShapeDtypeStruct((B, N_MEM, C_Q), BF16)
    spec = pl.BlockSpec((1, N_MEM, C_Q), lambda b: (b, 0, 0))
    return pl.pallas_call(
        _memkv_kernel,
        grid=(B,),
        in_specs=[pl.BlockSpec((1, N_MEM, D_MODEL), lambda b: (b, 0, 0)),
                  _const_spec((1, D_MODEL)), _const_spec((D_MODEL, 2 * C_Q))],
        out_specs=(spec, spec),
        out_shape=(out, out),
        compiler_params=_params(("parallel",)),
        name="memkv",
    )(mem, g, w)


def _merge_kernel(x_ref, oat_ref, obt_ref, qc_ref, mk_ref, mv_ref, g_ref, wgl_ref, wa_ref, wb_ref, wc_ref,
                  wout_ref, o_ref):
    x = x_ref[0]
    tm = x.shape[0]
    u = _rms(x, g_ref[...]).astype(BF16)

    oc = []
    for h in range(C_HEADS):
        sl = slice(h * C_HEAD_DIM, (h + 1) * C_HEAD_DIM)
        s = lax.dot_general(qc_ref[0, :, sl], mk_ref[0, :, sl], NT_DIMS,
                            preferred_element_type=F32) * (C_HEAD_DIM ** -0.5)
        e = jnp.exp(s - jnp.max(s, axis=-1, keepdims=True))
        pv = jnp.dot(e.astype(BF16), mv_ref[0, :, sl], preferred_element_type=F32)
        oc.append((pv / jnp.sum(e, axis=-1, keepdims=True)).astype(BF16))
    oc = jnp.concatenate(oc, axis=1)

    branches = (
        lax.dot_general(oat_ref[0].reshape(A_Q, tm), wa_ref[...], TN_DIMS, preferred_element_type=F32),
        lax.dot_general(obt_ref[0].reshape(B_OUT, tm), wb_ref[...], TN_DIMS, preferred_element_type=F32),
        jnp.dot(oc, wc_ref[...], preferred_element_type=F32),
    )
    merged = jnp.zeros(x.shape, F32)
    for i, br in enumerate(branches):
        gl = jnp.dot(u, wgl_ref[:, i * D_MODEL:(i + 1) * D_MODEL], preferred_element_type=F32)
        merged = merged + br * (1.0 / (1.0 + jnp.exp(-gl)))
    o_ref[0] = x + jnp.dot(merged.astype(BF16), wout_ref[...], preferred_element_type=F32)


def _merge(x1, oat, obt, qc, mk, mv, g, wgl, wa, wb, wc, wout):
    B, S, D = x1.shape
    tok = lambda w: pl.BlockSpec((1, TM_MERGE, w), lambda b, i: (b, i, 0))
    head_t = lambda heads, rows: pl.BlockSpec((1, heads, rows, TM_MERGE), lambda b, i: (b, 0, 0, i))
    mem = pl.BlockSpec((1, N_MEM, C_Q), lambda b, i: (b, 0, 0))
    return pl.pallas_call(
        _merge_kernel,
        grid=(B, S // TM_MERGE),
        in_specs=[
            tok(D), head_t(A_HEADS, A_HEAD_DIM), head_t(B_HEADS, B_V), tok(C_Q), mem, mem,
            _const_spec((1, D)), _const_spec((D, N_BRANCH * D)),
            _const_spec((A_Q, D)), _const_spec((B_OUT, D)), _const_spec((C_Q, D)),
            _const_spec((D, D)),
        ],
        out_specs=tok(D),
        out_shape=jax.ShapeDtypeStruct((B, S, D), F32),
        compiler_params=_params(("parallel", "parallel")),
        name="merge",
    )(x1, oat, obt, qc, mk, mv, g, wgl, wa, wb, wc, wout)


def _rope_tables(seq, half):
    inv = ROPE_THETA ** (-jnp.arange(0, 2 * half, 2, dtype=F32) / (2 * half))
    ang = inv[:, None] * jnp.arange(seq, dtype=F32)[None, :]
    return jnp.stack([jnp.cos(ang), jnp.sin(ang)])


def _prepare(ffn1_norm, ffn1_w_in, ffn1_w_out, mix_norm, w_in, mla_q_norm, mla_w_uq, mla_kv_norm,
             mla_w_ukv, attn_sink, mem_norm, w_mem_kv, w_branch_a, w_branch_b, w_branch_c, w_out,
             ffn2_norm, ffn2_w_in, ffn2_w_out, final_norm):
    row = lambda v: v.reshape(1, -1).astype(F32)
    cols, o = [], 0
    for s in IN_SIZES:
        cols.append(w_in[0][:, o:o + s])
        o += s
    w_qa, w_ka, w_va, w_cq, w_ckv, w_kr, w_qc, w_gl = cols
    w1 = jnp.concatenate([w_cq, w_ckv, w_qc], axis=1).astype(BF16)
    wt = jnp.concatenate([w_qa, w_va, w_ka, w_kr], axis=1).T.astype(BF16)
    wuq = mla_w_uq[0].reshape(B_Q_LORA, B_HEADS, B_NOPE + B_ROPE)
    wuqt = jnp.pad(wuq, ((0, 0), (0, 0), (0, B_SLAB - B_NOPE - B_ROPE))).reshape(B_Q_LORA, B_CAT).T.astype(BF16)
    wukv = mla_w_ukv[0].reshape(B_KV_LORA, B_HEADS, B_NOPE + B_V)
    wk = jnp.pad(wukv[:, :, :B_NOPE], ((0, 0), (0, 0), (0, B_SLAB - B_NOPE))).reshape(B_KV_LORA, B_CAT).astype(BF16)
    wvt = wukv[:, :, B_NOPE:].reshape(B_KV_LORA, B_OUT).T.astype(BF16)
    sink_lanes = jnp.repeat(attn_sink[0].astype(F32) * LOG2E, BLK).reshape(A_KV_HEADS, 1, A_GROUP * BLK)
    return dict(
        ffn1=(row(ffn1_norm[0]), ffn1_w_in[0].astype(BF16), ffn1_w_out[0].astype(BF16)),
        ffn2=(row(ffn2_norm[0]), ffn2_w_in[0].astype(BF16), ffn2_w_out[0].astype(BF16)),
        final=row(final_norm),
        inproj=(row(mix_norm[0]), w1, wt),
        mla=(row(mla_q_norm[0]), wuqt, row(mla_kv_norm[0]), wk, wvt),
        sink=sink_lanes,
        mem=(row(mem_norm[0]), w_mem_kv[0].astype(BF16)),
        merge=(row(mix_norm[0]), w_gl.astype(BF16), w_branch_a[0].astype(BF16), w_branch_b[0].astype(BF16),
               w_branch_c[0].astype(BF16), w_out[0].astype(BF16)),
    )


def _trunk(x, mem, w, tables):
    x1 = _ffn(x, *w["ffn1"], w["final"], final_norm=False)
    qat, ka, vat, qt, kcat, vt, qc = _inproj(x1, *w["inproj"], *tables, *w["mla"])
    oat = _attn_a(qat, ka, vat, w["sink"])
    obt = _mla(qt, kcat, vt)
    mk, mv = _memkv(mem, *w["mem"])
    x2 = _merge(x1, oat, obt, qc, mk, mv, *w["merge"])
    return _ffn(x2, *w["ffn2"], w["final"], final_norm=True)


def kernel(x_prompt, x_sample, mem_prompt, mem_sample, ffn1_norm, ffn1_w_in, ffn1_w_out, mix_norm, w_in,
           mla_q_norm, mla_w_uq, mla_kv_norm, mla_w_ukv, attn_sink, mem_norm, w_mem_kv, w_branch_a,
           w_branch_b, w_branch_c, w_out, ffn2_norm, ffn2_w_in, ffn2_w_out, final_norm):
    w = _prepare(ffn1_norm, ffn1_w_in, ffn1_w_out, mix_norm, w_in, mla_q_norm, mla_w_uq, mla_kv_norm,
                 mla_w_ukv, attn_sink, mem_norm, w_mem_kv, w_branch_a, w_branch_b, w_branch_c, w_out,
                 ffn2_norm, ffn2_w_in, ffn2_w_out, final_norm)
    seq = max(x_prompt.shape[1], x_sample.shape[1])
    tables = (_rope_tables(seq, A_HEAD_DIM // 2), _rope_tables(seq, B_ROPE // 2))
    return (_trunk(x_prompt, mem_prompt, w, tables), _trunk(x_sample, mem_sample, w, tables))
```

```python
import functools

import jax
import jax.numpy as jnp
from jax import lax
from jax.experimental import pallas as pl
from jax.experimental.pallas import tpu as pltpu

D_MODEL = 1024
N_MEM = 256
BLK = 128
WINDOW = 128
ROPE_THETA = 10000.0
EPS = 1e-6
NEG = -1e30
A_HEADS = 8
A_KV_HEADS = 2
A_GROUP = A_HEADS // A_KV_HEADS
A_HEAD_DIM = 64
A_Q = A_HEADS * A_HEAD_DIM
A_KV = A_KV_HEADS * A_HEAD_DIM
B_HEADS = 8
B_Q_LORA = 384
B_KV_LORA = 256
B_NOPE = 64
B_ROPE = 32
B_V = 64
B_OUT = B_HEADS * B_V
C_HEADS = 4
C_HEAD_DIM = 128
C_Q = C_HEADS * C_HEAD_DIM
N_BRANCH = 3
D_FF = 2816
IN_SIZES = (A_Q, A_KV, A_KV, B_Q_LORA, B_KV_LORA, B_ROPE, C_Q, N_BRANCH * D_MODEL)

LANES = 128
B_SLAB = LANES
B_CAT = B_HEADS * B_SLAB
OFF_CQ, OFF_CKV, OFF_QC = 0, 384, 640
W1_COLS = 1152
ROW_QA, ROW_VA, ROW_KA, ROW_KR = 0, 512, 640, 768
WT_ROWS = 800

TM_FFN = 512
FF_CHUNK = 512
TM_IN = 512
TA = 512
TQ = 512
TK = 4096
TKC = 256
MLA_ROUND_CHUNKS = 8
MLA_AHEAD = 2
MLA_BUFS = 4
PIPE_DEPTH = 3
V_EXT = 80
TM_MERGE = 512
VMEM_LIMIT = 52 * 1024 * 1024
LOG2E = 1.4426950408889634

BF16 = jnp.bfloat16
F32 = jnp.float32
NT_DIMS = (((1,), (1,)), ((), ()))
TN_DIMS = (((0,), (0,)), ((), ()))


def _rms(x, g):
    return x * lax.rsqrt(jnp.mean(x * x, axis=-1, keepdims=True) + EPS) * g


def _const_spec(shape):
    nd = len(shape)
    return pl.BlockSpec(shape, lambda *_: (0,) * nd, pipeline_mode=pl.Buffered(1))


def _params(sem):
    return pltpu.CompilerParams(dimension_semantics=sem, vmem_limit_bytes=VMEM_LIMIT)


def _ones_row_tile(width):
    return (lax.broadcasted_iota(jnp.int32, (V_EXT - B_V, width), 0) == 0).astype(BF16)


def _ffn_kernel(x_ref, g_ref, w1_ref, w2_ref, fn_ref, o_ref, *, final_norm):
    x = x_ref[0]
    xn = _rms(x, g_ref[...]).astype(BF16)
    acc = jnp.zeros(x.shape, F32)
    lo = 0
    while lo < D_FF:
        fc = min(FF_CHUNK, D_FF - lo)
        g = jnp.dot(xn, w1_ref[:, lo:lo + fc], preferred_element_type=F32)
        u = jnp.dot(xn, w1_ref[:, D_FF + lo:D_FF + lo + fc], preferred_element_type=F32)
        h = (g * (1.0 / (1.0 + jnp.exp(-g))) * u).astype(BF16)
        acc = acc + jnp.dot(h, w2_ref[lo:lo + fc, :], preferred_element_type=F32)
        lo += fc
    y = x + 0.5 * acc
    if final_norm:
        y = _rms(y, fn_ref[...])
    o_ref[0] = y


def _ffn(x, g, w1, w2, fn, final_norm):
    B, S, D = x.shape
    return pl.pallas_call(
        functools.partial(_ffn_kernel, final_norm=final_norm),
        grid=(B, S // TM_FFN),
        in_specs=[
            pl.BlockSpec((1, TM_FFN, D), lambda b, i: (b, i, 0)),
            _const_spec((1, D)),
            _const_spec((D, 2 * D_FF)),
            _const_spec((D_FF, D)),
            _const_spec((1, D)),
        ],
        out_specs=pl.BlockSpec((1, TM_FFN, D), lambda b, i: (b, i, 0)),
        out_shape=jax.ShapeDtypeStruct((B, S, D), F32),
        compiler_params=_params(("parallel", "parallel")),
        name="ffn_final" if final_norm else "ffn",
    )(x, g, w1, w2, fn)


def _rope_rows(x, tab_ref):
    half = x.shape[0] // 2
    x1, x2 = x[:half], x[half:]
    cos, sin = tab_ref[0], tab_ref[1]
    return jnp.concatenate([x1 * cos - x2 * sin, x2 * cos + x1 * sin], axis=0)


def _inproj_kernel(x_ref, g_ref, w1_ref, wt_ref, ta_ref, tb_ref, qn_ref, wuqt_ref, kvn_ref,
                   wk_ref, wvt_ref, qat_ref, ka_ref, vat_ref, qt_ref, kcat_ref, vt_ref, qc_ref):
    tm = x_ref.shape[1]
    u = _rms(x_ref[0], g_ref[...]).astype(BF16)
    z = jnp.dot(u, w1_ref[...], preferred_element_type=F32)
    zt = lax.dot_general(wt_ref[...], u, NT_DIMS, preferred_element_type=F32)

    scale_a = A_HEAD_DIM ** -0.5 * LOG2E
    zeros_half = jnp.zeros((A_HEAD_DIM, tm), BF16)
    for h in range(A_HEADS):
        rows = slice(ROW_QA + h * A_HEAD_DIM, ROW_QA + (h + 1) * A_HEAD_DIM)
        q = (_rope_rows(zt[rows, :], ta_ref) * scale_a).astype(BF16)
        g_kv = h // A_GROUP
        qat_ref[0, h, g_kv * A_HEAD_DIM:(g_kv + 1) * A_HEAD_DIM, :] = q
        qat_ref[0, h, (1 - g_kv) * A_HEAD_DIM:(2 - g_kv) * A_HEAD_DIM, :] = zeros_half
    ka_t = jnp.concatenate(
        [_rope_rows(zt[ROW_KA + g * A_HEAD_DIM:ROW_KA + (g + 1) * A_HEAD_DIM, :], ta_ref)
         for g in range(A_KV_HEADS)], axis=0)
    ka_ref[0] = ka_t.T.astype(BF16)
    ones_a = _ones_row_tile(tm)
    for g_kv in range(A_KV_HEADS):
        rows = slice(ROW_VA + g_kv * A_HEAD_DIM, ROW_VA + (g_kv + 1) * A_HEAD_DIM)
        vat_ref[0, g_kv, 0:A_HEAD_DIM, :] = zt[rows, :].astype(BF16)
        vat_ref[0, g_kv, A_HEAD_DIM:V_EXT, :] = ones_a

    qc_ref[0] = z[:, OFF_QC:OFF_QC + C_Q].astype(BF16)

    cqn = _rms(z[:, OFF_CQ:OFF_CQ + B_Q_LORA], qn_ref[...]).astype(BF16)
    qt = lax.dot_general(wuqt_ref[...], cqn, NT_DIMS, preferred_element_type=F32)
    scale_b = (B_NOPE + B_ROPE) ** -0.5 * LOG2E
    rope_lo, rope_hi = B_NOPE, B_NOPE + B_ROPE
    for h in range(B_HEADS):
        slab = qt[h * B_SLAB:(h + 1) * B_SLAB, :]
        slab = jnp.concatenate([slab[:rope_lo], _rope_rows(slab[rope_lo:rope_hi], tb_ref), slab[rope_hi:]], axis=0)
        qt_ref[0, h] = (slab * scale_b).astype(BF16)

    cn = _rms(z[:, OFF_CKV:OFF_CKV + B_KV_LORA], kvn_ref[...]).astype(BF16)
    kn = jnp.dot(cn, wk_ref[...], preferred_element_type=F32)
    kr_t = jnp.concatenate([jnp.zeros((rope_lo, tm), F32),
                            _rope_rows(zt[ROW_KR:ROW_KR + B_ROPE, :], tb_ref),
                            jnp.zeros((B_SLAB - rope_hi, tm), F32)], axis=0)
    kr = kr_t.T
    for h in range(B_HEADS):
        kcat_ref[0, h] = (kn[:, h * B_SLAB:(h + 1) * B_SLAB] + kr).astype(BF16)

    vt = lax.dot_general(wvt_ref[...], cn, NT_DIMS, preferred_element_type=F32)
    ones_b = _ones_row_tile(TKC)
    for c in range(tm // TKC):
        for h in range(B_HEADS):
            vt_ref[0, c, h, 0:B_V, :] = vt[h * B_V:(h + 1) * B_V, c * TKC:(c + 1) * TKC].astype(BF16)
            vt_ref[0, c, h, B_V:V_EXT, :] = ones_b


def _inproj(x1, g, w1, wt, tab_a, tab_b, qn, wuqt, kvn, wk, wvt):
    B, S, D = x1.shape
    tok = lambda w: pl.BlockSpec((1, TM_IN, w), lambda b, i: (b, i, 0))
    tab = lambda half: pl.BlockSpec((2, half, TM_IN), lambda b, i: (0, 0, i))
    head_t = lambda heads, rows: pl.BlockSpec((1, heads, rows, TM_IN), lambda b, i: (b, 0, 0, i))
    out_shape = (
        jax.ShapeDtypeStruct((B, A_HEADS, LANES, S), BF16),
        jax.ShapeDtypeStruct((B, S, A_KV), BF16),
        jax.ShapeDtypeStruct((B, A_KV_HEADS, V_EXT, S), BF16),
        jax.ShapeDtypeStruct((B, B_HEADS, B_SLAB, S), BF16),
        jax.ShapeDtypeStruct((B, B_HEADS, S, B_SLAB), BF16),
        jax.ShapeDtypeStruct((B, S // TKC, B_HEADS, V_EXT, TKC), BF16),
        jax.ShapeDtypeStruct((B, S, C_Q), BF16),
    )
    out_specs = (
        head_t(A_HEADS, LANES), tok(A_KV), head_t(A_KV_HEADS, V_EXT), head_t(B_HEADS, B_SLAB),
        pl.BlockSpec((1, B_HEADS, TM_IN, B_SLAB), lambda b, i: (b, 0, i, 0)),
        pl.BlockSpec((1, TM_IN // TKC, B_HEADS, V_EXT, TKC), lambda b, i: (b, i, 0, 0, 0)),
        tok(C_Q),
    )
    return pl.pallas_call(
        _inproj_kernel,
        grid=(B, S // TM_IN),
        in_specs=[
            tok(D), _const_spec((1, D)), _const_spec((D, W1_COLS)), _const_spec((WT_ROWS, D)),
            tab(A_HEAD_DIM // 2), tab(B_ROPE // 2),
            _const_spec((1, B_Q_LORA)), _const_spec((B_CAT, B_Q_LORA)),
            _const_spec((1, B_KV_LORA)), _const_spec((B_KV_LORA, B_CAT)),
            _const_spec((B_OUT, B_KV_LORA)),
        ],
        out_specs=out_specs,
        out_shape=out_shape,
        compiler_params=_params(("parallel", "parallel")),
        name="inproj",
    )(x1, g, w1, wt, tab_a, tab_b, qn, wuqt, kvn, wk, wvt)


def _attn_a_kernel(sink_ref, qt_ref, kp_ref, kc_ref, kn_ref, vp_ref, vc_ref, vn_ref, o_ref, *s_bufs, seq):
    i = pl.program_id(1)
    depth = len(s_bufs)
    row0 = pl.multiple_of(jnp.minimum(i, 0), 3 * BLK)
    k_all = jnp.concatenate([kp_ref[0], kc_ref[0], kn_ref[0]], axis=0)
    v_all = [jnp.concatenate([vp_ref[0, g], vc_ref[0, g], vn_ref[0, g]], axis=1)
             for g in range(A_KV_HEADS)]

    per = TA // BLK
    width = A_GROUP * BLK
    c = lax.broadcasted_iota(jnp.int32, (3 * BLK, width), 0)
    r = lax.broadcasted_iota(jnp.int32, (3 * BLK, width), 1) & (BLK - 1)
    bias_mid = jnp.where(jnp.abs(c - BLK - r) <= WINDOW, 0.0, NEG)
    bias_first = jnp.where(i * TA - BLK + c >= 0, bias_mid, NEG)
    bias_last = jnp.where(i * TA + (per - 2) * BLK + c < seq, bias_mid, NEG)
    biases = [bias_first] + [bias_mid] * (per - 2) + [bias_last]

    items = [(j, g) for j in range(per) for g in range(A_KV_HEADS)]

    def produce(n):
        j, g = items[n]
        cols = slice(j * BLK, (j + 1) * BLK)
        qt = jnp.concatenate([qt_ref[0, g * A_GROUP + hh, :, cols] for hh in range(A_GROUP)], axis=1)
        s = jnp.dot(k_all[j * BLK:(j + 3) * BLK], qt, preferred_element_type=F32) + biases[j]
        s_bufs[n % depth][...] = s
        return jnp.max(s, axis=0, keepdims=True)

    def consume(n, col_max):
        j, g = items[n]
        cols = slice(j * BLK, (j + 1) * BLK)
        sink = sink_ref[g]
        m = jnp.maximum(col_max, sink)
        p = jnp.exp2(s_bufs[n % depth][pl.ds(row0, 3 * BLK), :] - m).astype(BF16)
        pv = jnp.dot(v_all[g][:, j * BLK:(j + 3) * BLK], p, preferred_element_type=F32)
        denom = pv[A_HEAD_DIM:A_HEAD_DIM + 1, :] + jnp.exp2(sink - m)
        o = (pv[0:A_HEAD_DIM, :] / denom).astype(BF16)
        for hh in range(A_GROUP):
            o_ref[0, g * A_GROUP + hh, :, cols] = o[:, hh * BLK:(hh + 1) * BLK]

    col_max = {n: produce(n) for n in range(min(depth - 1, len(items)))}
    for n in range(len(items)):
        ahead = n + depth - 1
        if ahead < len(items):
            col_max[ahead] = produce(ahead)
        consume(n, col_max.pop(n))


def _attn_a(qat, ka, vat, sink_lanes):
    B, _, _, S = qat.shape
    nb = S // BLK
    per = TA // BLK
    prev_i = lambda i: jnp.maximum(i * per - 1, 0)
    next_i = lambda i: jnp.minimum((i + 1) * per, nb - 1)
    k_prev = pl.BlockSpec((1, BLK, A_KV), lambda b, i: (b, prev_i(i), 0))
    k_cur = pl.BlockSpec((1, TA, A_KV), lambda b, i: (b, i, 0))
    k_next = pl.BlockSpec((1, BLK, A_KV), lambda b, i: (b, next_i(i), 0))
    v_prev = pl.BlockSpec((1, A_KV_HEADS, V_EXT, BLK), lambda b, i: (b, 0, 0, prev_i(i)))
    v_cur = pl.BlockSpec((1, A_KV_HEADS, V_EXT, TA), lambda b, i: (b, 0, 0, i))
    v_next = pl.BlockSpec((1, A_KV_HEADS, V_EXT, BLK), lambda b, i: (b, 0, 0, next_i(i)))
    return pl.pallas_call(
        functools.partial(_attn_a_kernel, seq=S),
        grid=(B, S // TA),
        in_specs=[
            _const_spec((A_KV_HEADS, 1, A_GROUP * BLK)),
            pl.BlockSpec((1, A_HEADS, LANES, TA), lambda b, i: (b, 0, 0, i)),
            k_prev, k_cur, k_next, v_prev, v_cur, v_next,
        ],
        out_specs=pl.BlockSpec((1, A_HEADS, A_HEAD_DIM, TA), lambda b, i: (b, 0, 0, i)),
        out_shape=jax.ShapeDtypeStruct((B, A_HEADS, A_HEAD_DIM, S), BF16),
        scratch_shapes=[pltpu.VMEM((3 * BLK, A_GROUP * BLK), F32)] * PIPE_DEPTH,
        compiler_params=_params(("parallel", "parallel")),
        name="attn_a",
    )(sink_lanes, qat, ka, ka, ka, vat, vat, vat)


def _mla_kernel(qt_ref, k_ref, vt_ref, o_ref, m_ref, acc_ref, *s_bufs):
    j = pl.program_id(2)
    n_bufs = len(s_bufs)
    round_items = [(c, h) for c in range(MLA_ROUND_CHUNKS) for h in range(B_HEADS)]
    n_round = len(round_items)
    assert n_round % n_bufs == 0 and MLA_AHEAD < n_bufs
    last_chunk = TK // TKC - 1
    row0 = pl.multiple_of(jnp.minimum(j, 0), TKC)

    @pl.when(j == 0)
    def _():
        m_ref[...] = jnp.full(m_ref.shape, NEG, F32)
        acc_ref[...] = jnp.zeros(acc_ref.shape, F32)

    def produce(chunk, h, buf):
        rows = pl.ds(pl.multiple_of(chunk * TKC, TKC), TKC)
        s = jnp.dot(k_ref[0, h, rows, :], qt_ref[0, h], preferred_element_type=F32)
        s_bufs[buf][...] = s
        return jnp.max(s, axis=0, keepdims=True)

    def consume(chunk, h, buf, col_max):
        m_old = m_ref[h]
        m_new = jnp.maximum(m_old, col_max)
        p = jnp.exp2(s_bufs[buf][pl.ds(row0, TKC), :] - m_new).astype(BF16)
        pv = jnp.dot(vt_ref[0, chunk, h], p, preferred_element_type=F32)
        acc_ref[h] = jnp.exp2(m_old - m_new) * acc_ref[h] + pv
        m_ref[h] = m_new

    def one_round(r, in_flight):
        in_flight = list(in_flight)
        base = r * MLA_ROUND_CHUNKS
        for n, (c, h) in enumerate(round_items):
            a = n + MLA_AHEAD
            c_a, h_a = round_items[a % n_round]
            chunk_a = base + c_a
            if a >= n_round:
                chunk_a = jnp.minimum(chunk_a + MLA_ROUND_CHUNKS, last_chunk)
            in_flight.append(produce(chunk_a, h_a, a % n_bufs))
            consume(base + c, h, n % n_bufs, in_flight.pop(0))
        return tuple(in_flight)

    first = tuple(produce(round_items[n][0], round_items[n][1], n % n_bufs) for n in range(MLA_AHEAD))
    lax.fori_loop(0, TK // (MLA_ROUND_CHUNKS * TKC), one_round, first)

    @pl.when(j == pl.num_programs(2) - 1)
    def _():
        for h in range(B_HEADS):
            o_ref[0, h] = (acc_ref[h, 0:B_V, :] / acc_ref[h, B_V:B_V + 1, :]).astype(BF16)


def _mla(qt, kcat, vt):
    B, _, _, S = qt.shape
    return pl.pallas_call(
        _mla_kernel,
        grid=(B, S // TQ, S // TK),
        in_specs=[
            pl.BlockSpec((1, B_HEADS, B_SLAB, TQ), lambda b, i, j: (b, 0, 0, i)),
            pl.BlockSpec((1, B_HEADS, TK, B_SLAB), lambda b, i, j: (b, 0, j, 0)),
            pl.BlockSpec((1, TK // TKC, B_HEADS, V_EXT, TKC), lambda b, i, j: (b, j, 0, 0, 0)),
        ],
        out_specs=pl.BlockSpec((1, B_HEADS, B_V, TQ), lambda b, i, j: (b, 0, 0, i)),
        out_shape=jax.ShapeDtypeStruct((B, B_HEADS, B_V, S), BF16),
        scratch_shapes=[
            pltpu.VMEM((B_HEADS, 1, TQ), F32),
            pltpu.VMEM((B_HEADS, V_EXT, TQ), F32),
        ] + [pltpu.VMEM((TKC, TQ), F32)] * MLA_BUFS,
        compiler_params=_params(("parallel", "parallel", "arbitrary")),
        name="mla",
    )(qt, kcat, vt)


def _memkv_kernel(mem_ref, g_ref, w_ref, k_ref, v_ref):
    mn = _rms(mem_ref[0], g_ref[...]).astype(BF16)
    kv = jnp.dot(mn, w_ref[...], preferred_element_type=F32)
    k_ref[0] = kv[:, :C_Q].astype(BF16)
    v_ref[0] = kv[:, C_Q:].astype(BF16)


def _memkv(mem, g, w):
    B = mem.shape[0]
    out = jax.ShapeDtypeStruct((B, N_MEM, C_Q), BF16)
    spec = pl.BlockSpec((1, N_MEM, C_Q), lambda b: (b, 0, 0))
    return pl.pallas_call(
        _memkv_kernel,
        grid=(B,),
        in_specs=[pl.BlockSpec((1, N_MEM, D_MODEL), lambda b: (b, 0, 0)),
                  _const_spec((1, D_MODEL)), _const_spec((D_MODEL, 2 * C_Q))],
        out_specs=(spec, spec),
        out_shape=(out, out),
        compiler_params=_params(("parallel",)),
        name="memkv",
    )(mem, g, w)


def _merge_kernel(x_ref, oat_ref, obt_ref, qc_ref, mk_ref, mv_ref, g_ref, wgl_ref, wa_ref, wb_ref, wc_ref,
                  wout_ref, o_ref):
    x = x_ref[0]
    tm = x.shape[0]
    u = _rms(x, g_ref[...]).astype(BF16)

    oc = []
    for h in range(C_HEADS):
        sl = slice(h * C_HEAD_DIM, (h + 1) * C_HEAD_DIM)
        s = lax.dot_general(qc_ref[0, :, sl], mk_ref[0, :, sl], NT_DIMS,
                            preferred_element_type=F32) * (C_HEAD_DIM ** -0.5)
        e = jnp.exp(s - jnp.max(s, axis=-1, keepdims=True))
        pv = jnp.dot(e.astype(BF16), mv_ref[0, :, sl], preferred_element_type=F32)
        oc.append((pv / jnp.sum(e, axis=-1, keepdims=True)).astype(BF16))
    oc = jnp.concatenate(oc, axis=1)

    branches = (
        lax.dot_general(oat_ref[0].reshape(A_Q, tm), wa_ref[...], TN_DIMS, preferred_element_type=F32),
        lax.dot_general(obt_ref[0].reshape(B_OUT, tm), wb_ref[...], TN_DIMS, preferred_element_type=F32),
        jnp.dot(oc, wc_ref[...], preferred_element_type=F32),
    )
    merged = jnp.zeros(x.shape, F32)
    for i, br in enumerate(branches):
        gl = jnp.dot(u, wgl_ref[:, i * D_MODEL:(i + 1) * D_MODEL], preferred_element_type=F32)
        merged = merged + br * (1.0 / (1.0 + jnp.exp(-gl)))
    o_ref[0] = x + jnp.dot(merged.astype(BF16), wout_ref[...], preferred_element_type=F32)


def _merge(x1, oat, obt, qc, mk, mv, g, wgl, wa, wb, wc, wout):
    B, S, D = x1.shape
    tok = lambda w: pl.BlockSpec((1, TM_MERGE, w), lambda b, i: (b, i, 0))
    head_t = lambda heads, rows: pl.BlockSpec((1, heads, rows, TM_MERGE), lambda b, i: (b, 0, 0, i))
    mem = pl.BlockSpec((1, N_MEM, C_Q), lambda b, i: (b, 0, 0))
    return pl.pallas_call(
        _merge_kernel,
        grid=(B, S // TM_MERGE),
        in_specs=[
            tok(D), head_t(A_HEADS, A_HEAD_DIM), head_t(B_HEADS, B_V), tok(C_Q), mem, mem,
            _const_spec((1, D)), _const_spec((D, N_BRANCH * D)),
            _const_spec((A_Q, D)), _const_spec((B_OUT, D)), _const_spec((C_Q, D)),
            _const_spec((D, D)),
        ],
        out_specs=tok(D),
        out_shape=jax.ShapeDtypeStruct((B, S, D), F32),
        compiler_params=_params(("parallel", "parallel")),
        name="merge",
    )(x1, oat, obt, qc, mk, mv, g, wgl, wa, wb, wc, wout)


def _rope_tables(seq, half):
    inv = ROPE_THETA ** (-jnp.arange(0, 2 * half, 2, dtype=F32) / (2 * half))
    ang = inv[:, None] * jnp.arange(seq, dtype=F32)[None, :]
    return jnp.stack([jnp.cos(ang), jnp.sin(ang)])


def _prepare(ffn1_norm, ffn1_w_in, ffn1_w_out, mix_norm, w_in, mla_q_norm, mla_w_uq, mla_kv_norm,
             mla_w_ukv, attn_sink, mem_norm, w_mem_kv, w_branch_a, w_branch_b, w_branch_c, w_out,
             ffn2_norm, ffn2_w_in, ffn2_w_out, final_norm):
    row = lambda v: v.reshape(1, -1).astype(F32)
    cols, o = [], 0
    for s in IN_SIZES:
        cols.append(w_in[0][:, o:o + s])
        o += s
    w_qa, w_ka, w_va, w_cq, w_ckv, w_kr, w_qc, w_gl = cols
    w1 = jnp.concatenate([w_cq, w_ckv, w_qc], axis=1).astype(BF16)
    wt = jnp.concatenate([w_qa, w_va, w_ka, w_kr], axis=1).T.astype(BF16)
    wuq = mla_w_uq[0].reshape(B_Q_LORA, B_HEADS, B_NOPE + B_ROPE)
    wuqt = jnp.pad(wuq, ((0, 0), (0, 0), (0, B_SLAB - B_NOPE - B_ROPE))).reshape(B_Q_LORA, B_CAT).T.astype(BF16)
    wukv = mla_w_ukv[0].reshape(B_KV_LORA, B_HEADS, B_NOPE + B_V)
    wk = jnp.pad(wukv[:, :, :B_NOPE], ((0, 0), (0, 0), (0, B_SLAB - B_NOPE))).reshape(B_KV_LORA, B_CAT).astype(BF16)
    wvt = wukv[:, :, B_NOPE:].reshape(B_KV_LORA, B_OUT).T.astype(BF16)
    sink_lanes = jnp.repeat(attn_sink[0].astype(F32) * LOG2E, BLK).reshape(A_KV_HEADS, 1, A_GROUP * BLK)
    return dict(
        ffn1=(row(ffn1_norm[0]), ffn1_w_in[0].astype(BF16), ffn1_w_out[0].astype(BF16)),
        ffn2=(row(ffn2_norm[0]), ffn2_w_in[0].astype(BF16), ffn2_w_out[0].astype(BF16)),
        final=row(final_norm),
        inproj=(row(mix_norm[0]), w1, wt),
        mla=(row(mla_q_norm[0]), wuqt, row(mla_kv_norm[0]), wk, wvt),
        sink=sink_lanes,
        mem=(row(mem_norm[0]), w_mem_kv[0].astype(BF16)),
        merge=(row(mix_norm[0]), w_gl.astype(BF16), w_branch_a[0].astype(BF16), w_branch_b[0].astype(BF16),
               w_branch_c[0].astype(BF16), w_out[0].astype(BF16)),
    )


def _trunk(x, mem, w, tables):
    x1 = _ffn(x, *w["ffn1"], w["final"], final_norm=False)
    qat, ka, vat, qt, kcat, vt, qc = _inproj(x1, *w["inproj"], *tables, *w["mla"])
    oat = _attn_a(qat, ka, vat, w["sink"])
    obt = _mla(qt, kcat, vt)
    mk, mv = _memkv(mem, *w["mem"])
    x2 = _merge(x1, oat, obt, qc, mk, mv, *w["merge"])
    return _ffn(x2, *w["ffn2"], w["final"], final_norm=True)


def kernel(x_prompt, x_sample, mem_prompt, mem_sample, ffn1_norm, ffn1_w_in, ffn1_w_out, mix_norm, w_in,
           mla_q_norm, mla_w_uq, mla_kv_norm, mla_w_ukv, attn_sink, mem_norm, w_mem_kv, w_branch_a,
           w_branch_b, w_branch_c, w_out, ffn2_norm, ffn2_w_in, ffn2_w_out, final_norm):
    w = _prepare(ffn1_norm, ffn1_w_in, ffn1_w_out, mix_norm, w_in, mla_q_norm, mla_w_uq, mla_kv_norm,
                 mla_w_ukv, attn_sink, mem_norm, w_mem_kv, w_branch_a, w_branch_b, w_branch_c, w_out,
                 ffn2_norm, ffn2_w_in, ffn2_w_out, final_norm)
    seq = max(x_prompt.shape[1], x_sample.shape[1])
    tables = (_rope_tables(seq, A_HEAD_DIM // 2), _rope_tables(seq, B_ROPE // 2))
    return (_trunk(x_prompt, mem_prompt, w, tables), _trunk(x_sample, mem_sample, w, tables))
```

```python
import functools

import jax
import jax.numpy as jnp
from jax import lax
from jax.experimental import pallas as pl
from jax.experimental.pallas import tpu as pltpu

D_MODEL = 1024
N_MEM = 256
BLK = 128
WINDOW = 128
ROPE_THETA = 10000.0
EPS = 1e-6
NEG = -1e30
A_HEADS = 8
A_KV_HEADS = 2
A_GROUP = A_HEADS // A_KV_HEADS
A_HEAD_DIM = 64
A_Q = A_HEADS * A_HEAD_DIM
A_KV = A_KV_HEADS * A_HEAD_DIM
B_HEADS = 8
B_Q_LORA = 384
B_KV_LORA = 256
B_NOPE = 64
B_ROPE = 32
B_V = 64
B_OUT = B_HEADS * B_V
C_HEADS = 4
C_HEAD_DIM = 128
C_Q = C_HEADS * C_HEAD_DIM
N_BRANCH = 3
D_FF = 2816
IN_SIZES = (A_Q, A_KV, A_KV, B_Q_LORA, B_KV_LORA, B_ROPE, C_Q, N_BRANCH * D_MODEL)

LANES = 128
B_SLAB = LANES
B_CAT = B_HEADS * B_SLAB
OFF_CQ, OFF_CKV, OFF_QC = 0, 384, 640
W1_COLS = 1152
ROW_QA, ROW_VA, ROW_KA, ROW_KR = 0, 512, 640, 768
WT_ROWS = 800

SUB = 512
TM_FFN = 1024
FF_CHUNK = 512
TM_IN = 1024
TA = 1024
TQ = 512
TK = 2048
TKC = 256
PIPE_DEPTH = 3
V_EXT = 80
TM_MERGE = 1024
VMEM_LIMIT = 52 * 1024 * 1024
LOG2E = 1.4426950408889634

BF16 = jnp.bfloat16
F32 = jnp.float32
NT_DIMS = (((1,), (1,)), ((), ()))
TN_DIMS = (((0,), (0,)), ((), ()))


def _rms(x, g):
    return x * lax.rsqrt(jnp.mean(x * x, axis=-1, keepdims=True) + EPS) * g


def _const_spec(shape):
    nd = len(shape)
    return pl.BlockSpec(shape, lambda *_: (0,) * nd, pipeline_mode=pl.Buffered(1))


def _params(sem):
    return pltpu.CompilerParams(dimension_semantics=sem, vmem_limit_bytes=VMEM_LIMIT)


def _ones_row_tile(width):
    return (lax.broadcasted_iota(jnp.int32, (V_EXT - B_V, width), 0) == 0).astype(BF16)


def _sub_tiled(tile_fn, token_axes):
    def kernel(*refs):
        n_sub = refs[0].shape[token_axes[0][0]] * token_axes[0][1] // SUB
        for t in range(n_sub):
            views = []
            for ref, ax in zip(refs, token_axes):
                if ax is None:
                    views.append(ref)
                    continue
                axis, granule = ax
                idx = [slice(None)] * len(ref.shape)
                idx[axis] = pl.ds(t * SUB // granule, SUB // granule)
                views.append(ref.at[tuple(idx)])
            tile_fn(*views)
    return kernel


def _ffn_tile(x_ref, g_ref, w1_ref, w2_ref, fn_ref, o_ref, *, final_norm):
    x = x_ref[0]
    xn = _rms(x, g_ref[...]).astype(BF16)
    acc = jnp.zeros(x.shape, F32)
    lo = 0
    while lo < D_FF:
        fc = min(FF_CHUNK, D_FF - lo)
        g = jnp.dot(xn, w1_ref[:, lo:lo + fc], preferred_element_type=F32)
        u = jnp.dot(xn, w1_ref[:, D_FF + lo:D_FF + lo + fc], preferred_element_type=F32)
        h = (g * (1.0 / (1.0 + jnp.exp(-g))) * u).astype(BF16)
        acc = acc + jnp.dot(h, w2_ref[lo:lo + fc, :], preferred_element_type=F32)
        lo += fc
    y = x + 0.5 * acc
    if final_norm:
        y = _rms(y, fn_ref[...])
    o_ref[0] = y


def _ffn(x, g, w1, w2, fn, final_norm):
    B, S, D = x.shape
    return pl.pallas_call(
        _sub_tiled(functools.partial(_ffn_tile, final_norm=final_norm),
                   [(1, 1), None, None, None, None, (1, 1)]),
        grid=(B, S // TM_FFN),
        in_specs=[
            pl.BlockSpec((1, TM_FFN, D), lambda b, i: (b, i, 0)),
            _const_spec((1, D)),
            _const_spec((D, 2 * D_FF)),
            _const_spec((D_FF, D)),
            _const_spec((1, D)),
        ],
        out_specs=pl.BlockSpec((1, TM_FFN, D), lambda b, i: (b, i, 0)),
        out_shape=jax.ShapeDtypeStruct((B, S, D), F32),
        compiler_params=_params(("parallel", "parallel")),
        name="ffn_final" if final_norm else "ffn",
    )(x, g, w1, w2, fn)


def _rope_rows(x, tab_ref):
    half = x.shape[0] // 2
    x1, x2 = x[:half], x[half:]
    cos, sin = tab_ref[0], tab_ref[1]
    return jnp.concatenate([x1 * cos - x2 * sin, x2 * cos + x1 * sin], axis=0)


def _inproj_tile(x_ref, g_ref, w1_ref, wt_ref, ta_ref, tb_ref, qn_ref, wuqt_ref, kvn_ref,
                   wk_ref, wvt_ref, qat_ref, ka_ref, vat_ref, qt_ref, kcat_ref, vt_ref, qc_ref):
    tm = x_ref.shape[1]
    u = _rms(x_ref[0], g_ref[...]).astype(BF16)
    z = jnp.dot(u, w1_ref[...], preferred_element_type=F32)
    zt = lax.dot_general(wt_ref[...], u, NT_DIMS, preferred_element_type=F32)

    scale_a = A_HEAD_DIM ** -0.5 * LOG2E
    zeros_half = jnp.zeros((A_HEAD_DIM, tm), BF16)
    for h in range(A_HEADS):
        rows = slice(ROW_QA + h * A_HEAD_DIM, ROW_QA + (h + 1) * A_HEAD_DIM)
        q = (_rope_rows(zt[rows, :], ta_ref) * scale_a).astype(BF16)
        g_kv = h // A_GROUP
        qat_ref[0, h, g_kv * A_HEAD_DIM:(g_kv + 1) * A_HEAD_DIM, :] = q
        qat_ref[0, h, (1 - g_kv) * A_HEAD_DIM:(2 - g_kv) * A_HEAD_DIM, :] = zeros_half
    ka_t = jnp.concatenate(
        [_rope_rows(zt[ROW_KA + g * A_HEAD_DIM:ROW_KA + (g + 1) * A_HEAD_DIM, :], ta_ref)
         for g in range(A_KV_HEADS)], axis=0)
    ka_ref[0] = ka_t.T.astype(BF16)
    ones_a = _ones_row_tile(tm)
    for g_kv in range(A_KV_HEADS):
        rows = slice(ROW_VA + g_kv * A_HEAD_DIM, ROW_VA + (g_kv + 1) * A_HEAD_DIM)
        vat_ref[0, g_kv, 0:A_HEAD_DIM, :] = zt[rows, :].astype(BF16)
        vat_ref[0, g_kv, A_HEAD_DIM:V_EXT, :] = ones_a

    qc_ref[0] = z[:, OFF_QC:OFF_QC + C_Q].astype(BF16)

    cqn = _rms(z[:, OFF_CQ:OFF_CQ + B_Q_LORA], qn_ref[...]).astype(BF16)
    qt = lax.dot_general(wuqt_ref[...], cqn, NT_DIMS, preferred_element_type=F32)
    scale_b = (B_NOPE + B_ROPE) ** -0.5 * LOG2E
    rope_lo, rope_hi = B_NOPE, B_NOPE + B_ROPE
    for h in range(B_HEADS):
        slab = qt[h * B_SLAB:(h + 1) * B_SLAB, :]
        slab = jnp.concatenate([slab[:rope_lo], _rope_rows(slab[rope_lo:rope_hi], tb_ref), slab[rope_hi:]], axis=0)
        qt_ref[0, h] = (slab * scale_b).astype(BF16)

    cn = _rms(z[:, OFF_CKV:OFF_CKV + B_KV_LORA], kvn_ref[...]).astype(BF16)
    kn = jnp.dot(cn, wk_ref[...], preferred_element_type=F32)
    kr_t = jnp.concatenate([jnp.zeros((rope_lo, tm), F32),
                            _rope_rows(zt[ROW_KR:ROW_KR + B_ROPE, :], tb_ref),
                            jnp.zeros((B_SLAB - rope_hi, tm), F32)], axis=0)
    kr = kr_t.T
    for h in range(B_HEADS):
        kcat_ref[0, h] = (kn[:, h * B_SLAB:(h + 1) * B_SLAB] + kr).astype(BF16)

    vt = lax.dot_general(wvt_ref[...], cn, NT_DIMS, preferred_element_type=F32)
    ones_b = _ones_row_tile(TKC)
    for c in range(tm // TKC):
        for h in range(B_HEADS):
            vt_ref[0, c, h, 0:B_V, :] = vt[h * B_V:(h + 1) * B_V, c * TKC:(c + 1) * TKC].astype(BF16)
            vt_ref[0, c, h, B_V:V_EXT, :] = ones_b


def _inproj(x1, g, w1, wt, tab_a, tab_b, qn, wuqt, kvn, wk, wvt):
    B, S, D = x1.shape
    tok = lambda w: pl.BlockSpec((1, TM_IN, w), lambda b, i: (b, i, 0))
    tab = lambda half: pl.BlockSpec((2, half, TM_IN), lambda b, i: (0, 0, i))
    head_t = lambda heads, rows: pl.BlockSpec((1, heads, rows, TM_IN), lambda b, i: (b, 0, 0, i))
    out_shape = (
        jax.ShapeDtypeStruct((B, A_HEADS, LANES, S), BF16),
        jax.ShapeDtypeStruct((B, S, A_KV), BF16),
        jax.ShapeDtypeStruct((B, A_KV_HEADS, V_EXT, S), BF16),
        jax.ShapeDtypeStruct((B, B_HEADS, B_SLAB, S), BF16),
        jax.ShapeDtypeStruct((B, B_HEADS, S, B_SLAB), BF16),
        jax.ShapeDtypeStruct((B, S // TKC, B_HEADS, V_EXT, TKC), BF16),
        jax.ShapeDtypeStruct((B, S, C_Q), BF16),
    )
    out_specs = (
        head_t(A_HEADS, LANES), tok(A_KV), head_t(A_KV_HEADS, V_EXT), head_t(B_HEADS, B_SLAB),
        pl.BlockSpec((1, B_HEADS, TM_IN, B_SLAB), lambda b, i: (b, 0, i, 0)),
        pl.BlockSpec((1, TM_IN // TKC, B_HEADS, V_EXT, TKC), lambda b, i: (b, i, 0, 0, 0)),
        tok(C_Q),
    )
    return pl.pallas_call(
        _sub_tiled(_inproj_tile, [(1, 1), None, None, None, (2, 1), (2, 1), None, None, None, None, None,
                                  (3, 1), (1, 1), (3, 1), (3, 1), (2, 1), (1, TKC), (1, 1)]),
        grid=(B, S // TM_IN),
        in_specs=[
            tok(D), _const_spec((1, D)), _const_spec((D, W1_COLS)), _const_spec((WT_ROWS, D)),
            tab(A_HEAD_DIM // 2), tab(B_ROPE // 2),
            _const_spec((1, B_Q_LORA)), _const_spec((B_CAT, B_Q_LORA)),
            _const_spec((1, B_KV_LORA)), _const_spec((B_KV_LORA, B_CAT)),
            _const_spec((B_OUT, B_KV_LORA)),
        ],
        out_specs=out_specs,
        out_shape=out_shape,
        compiler_params=_params(("parallel", "parallel")),
        name="inproj",
    )(x1, g, w1, wt, tab_a, tab_b, qn, wuqt, kvn, wk, wvt)


def _attn_a_kernel(sink_ref, qt_ref, kp_ref, kc_ref, kn_ref, vp_ref, vc_ref, vn_ref, o_ref, *s_bufs, seq):
    i = pl.program_id(1)
    depth = len(s_bufs)
    row0 = pl.multiple_of(jnp.minimum(i, 0), 3 * BLK)
    k_all = jnp.concatenate([kp_ref[0], kc_ref[0], kn_ref[0]], axis=0)
    v_all = [jnp.concatenate([vp_ref[0, g], vc_ref[0, g], vn_ref[0, g]], axis=1)
             for g in range(A_KV_HEADS)]

    per = TA // BLK
    width = A_GROUP * BLK
    c = lax.broadcasted_iota(jnp.int32, (3 * BLK, width), 0)
    r = lax.broadcasted_iota(jnp.int32, (3 * BLK, width), 1) & (BLK - 1)
    bias_mid = jnp.where(jnp.abs(c - BLK - r) <= WINDOW, 0.0, NEG)
    bias_first = jnp.where(i * TA - BLK + c >= 0, bias_mid, NEG)
    bias_last = jnp.where(i * TA + (per - 2) * BLK + c < seq, bias_mid, NEG)
    biases = [bias_first] + [bias_mid] * (per - 2) + [bias_last]

    items = [(j, g) for j in range(per) for g in range(A_KV_HEADS)]

    def produce(n):
        j, g = items[n]
        cols = slice(j * BLK, (j + 1) * BLK)
        qt = jnp.concatenate([qt_ref[0, g * A_GROUP + hh, :, cols] for hh in range(A_GROUP)], axis=1)
        s = jnp.dot(k_all[j * BLK:(j + 3) * BLK], qt, preferred_element_type=F32) + biases[j]
        s_bufs[n % depth][...] = s
        return jnp.max(s, axis=0, keepdims=True)

    def consume(n, col_max):
        j, g = items[n]
        cols = slice(j * BLK, (j + 1) * BLK)
        sink = sink_ref[g]
        m = jnp.maximum(col_max, sink)
        p = jnp.exp2(s_bufs[n % depth][pl.ds(row0, 3 * BLK), :] - m).astype(BF16)
        pv = jnp.dot(v_all[g][:, j * BLK:(j + 3) * BLK], p, preferred_element_type=F32)
        denom = pv[A_HEAD_DIM:A_HEAD_DIM + 1, :] + jnp.exp2(sink - m)
        o = (pv[0:A_HEAD_DIM, :] / denom).astype(BF16)
        for hh in range(A_GROUP):
            o_ref[0, g * A_GROUP + hh, :, cols] = o[:, hh * BLK:(hh + 1) * BLK]

    col_max = {n: produce(n) for n in range(min(depth - 1, len(items)))}
    for n in range(len(items)):
        ahead = n + depth - 1
        if ahead < len(items):
            col_max[ahead] = produce(ahead)
        consume(n, col_max.pop(n))


def _attn_a(qat, ka, vat, sink_lanes):
    B, _, _, S = qat.shape
    nb = S // BLK
    per = TA // BLK
    prev_i = lambda i: jnp.maximum(i * per - 1, 0)
    next_i = lambda i: jnp.minimum((i + 1) * per, nb - 1)
    k_prev = pl.BlockSpec((1, BLK, A_KV), lambda b, i: (b, prev_i(i), 0))
    k_cur = pl.BlockSpec((1, TA, A_KV), lambda b, i: (b, i, 0))
    k_next = pl.BlockSpec((1, BLK, A_KV), lambda b, i: (b, next_i(i), 0))
    v_prev = pl.BlockSpec((1, A_KV_HEADS, V_EXT, BLK), lambda b, i: (b, 0, 0, prev_i(i)))
    v_cur = pl.BlockSpec((1, A_KV_HEADS, V_EXT, TA), lambda b, i: (b, 0, 0, i))
    v_next = pl.BlockSpec((1, A_KV_HEADS, V_EXT, BLK), lambda b, i: (b, 0, 0, next_i(i)))
    return pl.pallas_call(
        functools.partial(_attn_a_kernel, seq=S),
        grid=(B, S // TA),
        in_specs=[
            _const_spec((A_KV_HEADS, 1, A_GROUP * BLK)),
            pl.BlockSpec((1, A_HEADS, LANES, TA), lambda b, i: (b, 0, 0, i)),
            k_prev, k_cur, k_next, v_prev, v_cur, v_next,
        ],
        out_specs=pl.BlockSpec((1, A_HEADS, A_HEAD_DIM, TA), lambda b, i: (b, 0, 0, i)),
        out_shape=jax.ShapeDtypeStruct((B, A_HEADS, A_HEAD_DIM, S), BF16),
        scratch_shapes=[pltpu.VMEM((3 * BLK, A_GROUP * BLK), F32)] * PIPE_DEPTH,
        compiler_params=_params(("parallel", "parallel")),
        name="attn_a",
    )(sink_lanes, qat, ka, ka, ka, vat, vat, vat)


def _mla_kernel(qt_ref, k_ref, vt_ref, o_ref, m_ref, acc_ref, *s_bufs):
    j = pl.program_id(2)
    items = [(c, h) for c in range(TK // TKC) for h in range(B_HEADS)]
    depth = len(s_bufs)
    row0 = pl.multiple_of(jnp.minimum(j, 0), TKC)

    @pl.when(j == 0)
    def _():
        m_ref[...] = jnp.full(m_ref.shape, NEG, F32)
        acc_ref[...] = jnp.zeros(acc_ref.shape, F32)

    def produce(n):
        c, h = items[n]
        s = jnp.dot(k_ref[0, h, c * TKC:(c + 1) * TKC, :], qt_ref[0, h],
                    preferred_element_type=F32)
        s_bufs[n % depth][...] = s
        return jnp.max(s, axis=0, keepdims=True)

    def consume(n, col_max):
        c, h = items[n]
        m_old = m_ref[h]
        m_new = jnp.maximum(m_old, col_max)
        p = jnp.exp2(s_bufs[n % depth][pl.ds(row0, TKC), :] - m_new).astype(BF16)
        pv = jnp.dot(vt_ref[0, c, h], p, preferred_element_type=F32)
        acc_ref[h] = jnp.exp2(m_old - m_new) * acc_ref[h] + pv
        m_ref[h] = m_new

    col_max = {n: produce(n) for n in range(min(depth - 1, len(items)))}
    for n in range(len(items)):
        ahead = n + depth - 1
        if ahead < len(items):
            col_max[ahead] = produce(ahead)
        consume(n, col_max.pop(n))

    @pl.when(j == pl.num_programs(2) - 1)
    def _():
        for h in range(B_HEADS):
            o_ref[0, h] = (acc_ref[h, 0:B_V, :] / acc_ref[h, B_V:B_V + 1, :]).astype(BF16)


def _mla(qt, kcat, vt):
    B, _, _, S = qt.shape
    return pl.pallas_call(
        _mla_kernel,
        grid=(B, S // TQ, S // TK),
        in_specs=[
            pl.BlockSpec((1, B_HEADS, B_SLAB, TQ), lambda b, i, j: (b, 0, 0, i)),
            pl.BlockSpec((1, B_HEADS, TK, B_SLAB), lambda b, i, j: (b, 0, j, 0)),
            pl.BlockSpec((1, TK // TKC, B_HEADS, V_EXT, TKC), lambda b, i, j: (b, j, 0, 0, 0)),
        ],
        out_specs=pl.BlockSpec((1, B_HEADS, B_V, TQ), lambda b, i, j: (b, 0, 0, i)),
        out_shape=jax.ShapeDtypeStruct((B, B_HEADS, B_V, S), BF16),
        scratch_shapes=[
            pltpu.VMEM((B_HEADS, 1, TQ), F32),
            pltpu.VMEM((B_HEADS, V_EXT, TQ), F32),
        ] + [pltpu.VMEM((TKC, TQ), F32)] * PIPE_DEPTH,
        compiler_params=_params(("parallel", "parallel", "arbitrary")),
        name="mla",
    )(qt, kcat, vt)


def _memkv_kernel(mem_ref, g_ref, w_ref, k_ref, v_ref):
    mn = _rms(mem_ref[0], g_ref[...]).astype(BF16)
    kv = jnp.dot(mn, w_ref[...], preferred_element_type=F32)
    k_ref[0] = kv[:, :C_Q].astype(BF16)
    v_ref[0] = kv[:, C_Q:].astype(BF16)


def _memkv(mem, g, w):
    B = mem.shape[0]
    out = jax.ShapeDtypeStruct((B, N_MEM, C_Q), BF16)
    spec = pl.BlockSpec((1, N_MEM, C_Q), lambda b: (b, 0, 0))
    return pl.pallas_call(
        _memkv_kernel,
        grid=(B,),
        in_specs=[pl.BlockSpec((1, N_MEM, D_MODEL), lambda b: (b, 0, 0)),
                  _const_spec((1, D_MODEL)), _const_spec((D_MODEL, 2 * C_Q))],
        out_specs=(spec, spec),
        out_shape=(out, out),
        compiler_params=_params(("parallel",)),
        name="memkv",
    )(mem, g, w)


def _merge_tile(x_ref, oat_ref, obt_ref, qc_ref, mk_ref, mv_ref, g_ref, wgl_ref, wa_ref, wb_ref, wc_ref,
                  wout_ref, o_ref):
    x = x_ref[0]
    tm = x.shape[0]
    u = _rms(x, g_ref[...]).astype(BF16)

    oc = []
    for h in range(C_HEADS):
        sl = slice(h * C_HEAD_DIM, (h + 1) * C_HEAD_DIM)
        s = lax.dot_general(qc_ref[0, :, sl], mk_ref[0, :, sl], NT_DIMS,
                            preferred_element_type=F32) * (C_HEAD_DIM ** -0.5)
        e = jnp.exp(s - jnp.max(s, axis=-1, keepdims=True))
        pv = jnp.dot(e.astype(BF16), mv_ref[0, :, sl], preferred_element_type=F32)
        oc.append((pv / jnp.sum(e, axis=-1, keepdims=True)).astype(BF16))
    oc = jnp.concatenate(oc, axis=1)

    branches = (
        lax.dot_general(oat_ref[0].reshape(A_Q, tm), wa_ref[...], TN_DIMS, preferred_element_type=F32),
        lax.dot_general(obt_ref[0].reshape(B_OUT, tm), wb_ref[...], TN_DIMS, preferred_element_type=F32),
        jnp.dot(oc, wc_ref[...], preferred_element_type=F32),
    )
    merged = jnp.zeros(x.shape, F32)
    for i, br in enumerate(branches):
        gl = jnp.dot(u, wgl_ref[:, i * D_MODEL:(i + 1) * D_MODEL], preferred_element_type=F32)
        merged = merged + br * (1.0 / (1.0 + jnp.exp(-gl)))
    o_ref[0] = x + jnp.dot(merged.astype(BF16), wout_ref[...], preferred_element_type=F32)


def _merge(x1, oat, obt, qc, mk, mv, g, wgl, wa, wb, wc, wout):
    B, S, D = x1.shape
    tok = lambda w: pl.BlockSpec((1, TM_MERGE, w), lambda b, i: (b, i, 0))
    head_t = lambda heads, rows: pl.BlockSpec((1, heads, rows, TM_MERGE), lambda b, i: (b, 0, 0, i))
    mem = pl.BlockSpec((1, N_MEM, C_Q), lambda b, i: (b, 0, 0))
    return pl.pallas_call(
        _sub_tiled(_merge_tile, [(1, 1), (3, 1), (3, 1), (1, 1)] + [None] * 8 + [(1, 1)]),
        grid=(B, S // TM_MERGE),
        in_specs=[
            tok(D), head_t(A_HEADS, A_HEAD_DIM), head_t(B_HEADS, B_V), tok(C_Q), mem, mem,
            _const_spec((1, D)), _const_spec((D, N_BRANCH * D)),
            _const_spec((A_Q, D)), _const_spec((B_OUT, D)), _const_spec((C_Q, D)),
            _const_spec((D, D)),
        ],
        out_specs=tok(D),
        out_shape=jax.ShapeDtypeStruct((B, S, D), F32),
        compiler_params=_params(("parallel", "parallel")),
        name="merge",
    )(x1, oat, obt, qc, mk, mv, g, wgl, wa, wb, wc, wout)


def _rope_tables(seq, half):
    inv = ROPE_THETA ** (-jnp.arange(0, 2 * half, 2, dtype=F32) / (2 * half))
    ang = inv[:, None] * jnp.arange(seq, dtype=F32)[None, :]
    return jnp.stack([jnp.cos(ang), jnp.sin(ang)])


def _prepare(ffn1_norm, ffn1_w_in, ffn1_w_out, mix_norm, w_in, mla_q_norm, mla_w_uq, mla_kv_norm,
             mla_w_ukv, attn_sink, mem_norm, w_mem_kv, w_branch_a, w_branch_b, w_branch_c, w_out,
             ffn2_norm, ffn2_w_in, ffn2_w_out, final_norm):
    row = lambda v: v.reshape(1, -1).astype(F32)
    cols, o = [], 0
    for s in IN_SIZES:
        cols.append(w_in[0][:, o:o + s])
        o += s
    w_qa, w_ka, w_va, w_cq, w_ckv, w_kr, w_qc, w_gl = cols
    w1 = jnp.concatenate([w_cq, w_ckv, w_qc], axis=1).astype(BF16)
    wt = jnp.concatenate([w_qa, w_va, w_ka, w_kr], axis=1).T.astype(BF16)
    wuq = mla_w_uq[0].reshape(B_Q_LORA, B_HEADS, B_NOPE + B_ROPE)
    wuqt = jnp.pad(wuq, ((0, 0), (0, 0), (0, B_SLAB - B_NOPE - B_ROPE))).reshape(B_Q_LORA, B_CAT).T.astype(BF16)
    wukv = mla_w_ukv[0].reshape(B_KV_LORA, B_HEADS, B_NOPE + B_V)
    wk = jnp.pad(wukv[:, :, :B_NOPE], ((0, 0), (0, 0), (0, B_SLAB - B_NOPE))).reshape(B_KV_LORA, B_CAT).astype(BF16)
    wvt = wukv[:, :, B_NOPE:].reshape(B_KV_LORA, B_OUT).T.astype(BF16)
    sink_lanes = jnp.repeat(attn_sink[0].astype(F32) * LOG2E, BLK).reshape(A_KV_HEADS, 1, A_GROUP * BLK)
    return dict(
        ffn1=(row(ffn1_norm[0]), ffn1_w_in[0].astype(BF16), ffn1_w_out[0].astype(BF16)),
        ffn2=(row(ffn2_norm[0]), ffn2_w_in[0].astype(BF16), ffn2_w_out[0].astype(BF16)),
        final=row(final_norm),
        inproj=(row(mix_norm[0]), w1, wt),
        mla=(row(mla_q_norm[0]), wuqt, row(mla_kv_norm[0]), wk, wvt),
        sink=sink_lanes,
        mem=(row(mem_norm[0]), w_mem_kv[0].astype(BF16)),
        merge=(row(mix_norm[0]), w_gl.astype(BF16), w_branch_a[0].astype(BF16), w_branch_b[0].astype(BF16),
               w_branch_c[0].astype(BF16), w_out[0].astype(BF16)),
    )


def _trunk(x, mem, w, tables):
    x1 = _ffn(x, *w["ffn1"], w["final"], final_norm=False)
    qat, ka, vat, qt, kcat, vt, qc = _inproj(x1, *w["inproj"], *tables, *w["mla"])
    oat = _attn_a(qat, ka, vat, w["sink"])
    obt = _mla(qt, kcat, vt)
    mk, mv = _memkv(mem, *w["mem"])
    x2 = _merge(x1, oat, obt, qc, mk, mv, *w["merge"])
    return _ffn(x2, *w["ffn2"], w["final"], final_norm=True)


def kernel(x_prompt, x_sample, mem_prompt, mem_sample, ffn1_norm, ffn1_w_in, ffn1_w_out, mix_norm, w_in,
           mla_q_norm, mla_w_uq, mla_kv_norm, mla_w_ukv, attn_sink, mem_norm, w_mem_kv, w_branch_a,
           w_branch_b, w_branch_c, w_out, ffn2_norm, ffn2_w_in, ffn2_w_out, final_norm):
    w = _prepare(ffn1_norm, ffn1_w_in, ffn1_w_out, mix_norm, w_in, mla_q_norm, mla_w_uq, mla_kv_norm,
                 mla_w_ukv, attn_sink, mem_norm, w_mem_kv, w_branch_a, w_branch_b, w_branch_c, w_out,
                 ffn2_norm, ffn2_w_in, ffn2_w_out, final_norm)
    seq = max(x_prompt.shape[1], x_sample.shape[1])
    tables = (_rope_tables(seq, A_HEAD_DIM // 2), _rope_tables(seq, B_ROPE // 2))
    return (_trunk(x_prompt, mem_prompt, w, tables), _trunk(x_sample, mem_sample, w, tables))
```

```python
import functools

import jax
import jax.numpy as jnp
from jax import lax
from jax.experimental import pallas as pl
from jax.experimental.pallas import tpu as pltpu

D_MODEL = 1024
N_MEM = 256
BLK = 128
WINDOW = 128
ROPE_THETA = 10000.0
EPS = 1e-6
NEG = -1e30
A_HEADS = 8
A_KV_HEADS = 2
A_GROUP = A_HEADS // A_KV_HEADS
A_HEAD_DIM = 64
A_Q = A_HEADS * A_HEAD_DIM
A_KV = A_KV_HEADS * A_HEAD_DIM
B_HEADS = 8
B_Q_LORA = 384
B_KV_LORA = 256
B_NOPE = 64
B_ROPE = 32
B_V = 64
B_OUT = B_HEADS * B_V
C_HEADS = 4
C_HEAD_DIM = 128
C_Q = C_HEADS * C_HEAD_DIM
N_BRANCH = 3
D_FF = 2816
IN_SIZES = (A_Q, A_KV, A_KV, B_Q_LORA, B_KV_LORA, B_ROPE, C_Q, N_BRANCH * D_MODEL)

LANES = 128
B_SLAB = LANES
B_CAT = B_HEADS * B_SLAB
OFF_CQ, OFF_CKV, OFF_QC = 0, 384, 640
W1_COLS = 1152
ROW_QA, ROW_VA, ROW_KA, ROW_KR = 0, 512, 640, 768
WT_ROWS = 800

SUB = 512
TM_FFN = 1024
FF_CHUNK = 512
TM_IN = 1024
TA = 1024
TQ = 512
TK = 2048
TKC = 256
PIPE_DEPTH = 3
V_EXT = 80
TM_MERGE = 1024
VMEM_LIMIT = 52 * 1024 * 1024
LOG2E = 1.4426950408889634

BF16 = jnp.bfloat16
F32 = jnp.float32
NT_DIMS = (((1,), (1,)), ((), ()))
TN_DIMS = (((0,), (0,)), ((), ()))


def _rms(x, g):
    return x * lax.rsqrt(jnp.mean(x * x, axis=-1, keepdims=True) + EPS) * g


def _const_spec(shape):
    nd = len(shape)
    return pl.BlockSpec(shape, lambda *_: (0,) * nd, pipeline_mode=pl.Buffered(1))


def _params(sem):
    return pltpu.CompilerParams(dimension_semantics=sem, vmem_limit_bytes=VMEM_LIMIT)


def _ones_row_tile(width):
    return (lax.broadcasted_iota(jnp.int32, (V_EXT - B_V, width), 0) == 0).astype(BF16)


def _sub_tiled(tile_fn, token_axes):
    def kernel(*refs):
        n_sub = refs[0].shape[token_axes[0][0]] * token_axes[0][1] // SUB
        for t in range(n_sub):
            views = []
            for ref, ax in zip(refs, token_axes):
                if ax is None:
                    views.append(ref)
                    continue
                axis, granule = ax
                idx = [slice(None)] * len(ref.shape)
                idx[axis] = pl.ds(t * SUB // granule, SUB // granule)
                views.append(ref.at[tuple(idx)])
            tile_fn(*views)
    return kernel


def _ffn_tile(x_ref, g_ref, w1_ref, w2_ref, fn_ref, o_ref, *, final_norm):
    x = x_ref[0]
    xn = _rms(x, g_ref[...]).astype(BF16)
    acc = jnp.zeros(x.shape, F32)
    lo = 0
    while lo < D_FF:
        fc = min(FF_CHUNK, D_FF - lo)
        g = jnp.dot(xn, w1_ref[:, lo:lo + fc], preferred_element_type=F32)
        u = jnp.dot(xn, w1_ref[:, D_FF + lo:D_FF + lo + fc], preferred_element_type=F32)
        h = (g * (1.0 / (1.0 + jnp.exp(-g))) * u).astype(BF16)
        acc = acc + jnp.dot(h, w2_ref[lo:lo + fc, :], preferred_element_type=F32)
        lo += fc
    y = x + 0.5 * acc
    if final_norm:
        y = _rms(y, fn_ref[...])
    o_ref[0] = y


def _ffn(x, g, w1, w2, fn, final_norm):
    B, S, D = x.shape
    return pl.pallas_call(
        _sub_tiled(functools.partial(_ffn_tile, final_norm=final_norm),
                   [(1, 1), None, None, None, None, (1, 1)]),
        grid=(B, S // TM_FFN),
        in_specs=[
            pl.BlockSpec((1, TM_FFN, D), lambda b, i: (b, i, 0)),
            _const_spec((1, D)),
            _const_spec((D, 2 * D_FF)),
            _const_spec((D_FF, D)),
            _const_spec((1, D)),
        ],
        out_specs=pl.BlockSpec((1, TM_FFN, D), lambda b, i: (b, i, 0)),
        out_shape=jax.ShapeDtypeStruct((B, S, D), F32),
        compiler_params=_params(("parallel", "parallel")),
        name="ffn_final" if final_norm else "ffn",
    )(x, g, w1, w2, fn)


def _rope_rows(x, tab_ref):
    half = x.shape[0] // 2
    x1, x2 = x[:half], x[half:]
    cos, sin = tab_ref[0], tab_ref[1]
    return jnp.concatenate([x1 * cos - x2 * sin, x2 * cos + x1 * sin], axis=0)


def _inproj_tile(x_ref, g_ref, w1_ref, wt_ref, ta_ref, tb_ref, qn_ref, wuqt_ref, kvn_ref,
                   wk_ref, wvt_ref, qat_ref, ka_ref, vat_ref, qt_ref, kcat_ref, vt_ref, qc_ref):
    tm = x_ref.shape[1]
    u = _rms(x_ref[0], g_ref[...]).astype(BF16)
    z = jnp.dot(u, w1_ref[...], preferred_element_type=F32)
    zt = lax.dot_general(wt_ref[...], u, NT_DIMS, preferred_element_type=F32)

    scale_a = A_HEAD_DIM ** -0.5 * LOG2E
    zeros_half = jnp.zeros((A_HEAD_DIM, tm), BF16)
    for h in range(A_HEADS):
        rows = slice(ROW_QA + h * A_HEAD_DIM, ROW_QA + (h + 1) * A_HEAD_DIM)
        q = (_rope_rows(zt[rows, :], ta_ref) * scale_a).astype(BF16)
        g_kv = h // A_GROUP
        qat_ref[0, h, g_kv * A_HEAD_DIM:(g_kv + 1) * A_HEAD_DIM, :] = q
        qat_ref[0, h, (1 - g_kv) * A_HEAD_DIM:(2 - g_kv) * A_HEAD_DIM, :] = zeros_half
    ka_t = jnp.concatenate(
        [_rope_rows(zt[ROW_KA + g * A_HEAD_DIM:ROW_KA + (g + 1) * A_HEAD_DIM, :], ta_ref)
         for g in range(A_KV_HEADS)], axis=0)
    ka_ref[0] = ka_t.T.astype(BF16)
    ones_a = _ones_row_tile(tm)
    for g_kv in range(A_KV_HEADS):
        rows = slice(ROW_VA + g_kv * A_HEAD_DIM, ROW_VA + (g_kv + 1) * A_HEAD_DIM)
        vat_ref[0, g_kv, 0:A_HEAD_DIM, :] = zt[rows, :].astype(BF16)
        vat_ref[0, g_kv, A_HEAD_DIM:V_EXT, :] = ones_a

    qc_ref[0] = z[:, OFF_QC:OFF_QC + C_Q].astype(BF16)

    cqn = _rms(z[:, OFF_CQ:OFF_CQ + B_Q_LORA], qn_ref[...]).astype(BF16)
    qt = lax.dot_general(wuqt_ref[...], cqn, NT_DIMS, preferred_element_type=F32)
    scale_b = (B_NOPE + B_ROPE) ** -0.5 * LOG2E
    rope_lo, rope_hi = B_NOPE, B_NOPE + B_ROPE
    for h in range(B_HEADS):
        slab = qt[h * B_SLAB:(h + 1) * B_SLAB, :]
        slab = jnp.concatenate([slab[:rope_lo], _rope_rows(slab[rope_lo:rope_hi], tb_ref), slab[rope_hi:]], axis=0)
        qt_ref[0, h] = (slab * scale_b).astype(BF16)

    cn = _rms(z[:, OFF_CKV:OFF_CKV + B_KV_LORA], kvn_ref[...]).astype(BF16)
    kn = jnp.dot(cn, wk_ref[...], preferred_element_type=F32)
    kr_t = jnp.concatenate([jnp.zeros((rope_lo, tm), F32),
                            _rope_rows(zt[ROW_KR:ROW_KR + B_ROPE, :], tb_ref),
                            jnp.zeros((B_SLAB - rope_hi, tm), F32)], axis=0)
    kr = kr_t.T
    for h in range(B_HEADS):
        kcat_ref[0, h] = (kn[:, h * B_SLAB:(h + 1) * B_SLAB] + kr).astype(BF16)

    vt = lax.dot_general(wvt_ref[...], cn, NT_DIMS, preferred_element_type=F32)
    ones_b = _ones_row_tile(TKC)
    for c in range(tm // TKC):
        for h in range(B_HEADS):
            vt_ref[0, c, h, 0:B_V, :] = vt[h * B_V:(h + 1) * B_V, c * TKC:(c + 1) * TKC].astype(BF16)
            vt_ref[0, c, h, B_V:V_EXT, :] = ones_b


def _inproj(x1, g, w1, wt, tab_a, tab_b, qn, wuqt, kvn, wk, wvt):
    B, S, D = x1.shape
    tok = lambda w: pl.BlockSpec((1, TM_IN, w), lambda b, i: (b, i, 0))
    tab = lambda half: pl.BlockSpec((2, half, TM_IN), lambda b, i: (0, 0, i))
    head_t = lambda heads, rows: pl.BlockSpec((1, heads, rows, TM_IN), lambda b, i: (b, 0, 0, i))
    out_shape = (
        jax.ShapeDtypeStruct((B, A_HEADS, LANES, S), BF16),
        jax.ShapeDtypeStruct((B, S, A_KV), BF16),
        jax.ShapeDtypeStruct((B, A_KV_HEADS, V_EXT, S), BF16),
        jax.ShapeDtypeStruct((B, B_HEADS, B_SLAB, S), BF16),
        jax.ShapeDtypeStruct((B, B_HEADS, S, B_SLAB), BF16),
        jax.ShapeDtypeStruct((B, S // TKC, B_HEADS, V_EXT, TKC), BF16),
        jax.ShapeDtypeStruct((B, S, C_Q), BF16),
    )
    out_specs = (
        head_t(A_HEADS, LANES), tok(A_KV), head_t(A_KV_HEADS, V_EXT), head_t(B_HEADS, B_SLAB),
        pl.BlockSpec((1, B_HEADS, TM_IN, B_SLAB), lambda b, i: (b, 0, i, 0)),
        pl.BlockSpec((1, TM_IN // TKC, B_HEADS, V_EXT, TKC), lambda b, i: (b, i, 0, 0, 0)),
        tok(C_Q),
    )
    return pl.pallas_call(
        _sub_tiled(_inproj_tile, [(1, 1), None, None, None, (2, 1), (2, 1), None, None, None, None, None,
                                  (3, 1), (1, 1), (3, 1), (3, 1), (2, 1), (1, TKC), (1, 1)]),
        grid=(B, S // TM_IN),
        in_specs=[
            tok(D), _const_spec((1, D)), _const_spec((D, W1_COLS)), _const_spec((WT_ROWS, D)),
            tab(A_HEAD_DIM // 2), tab(B_ROPE // 2),
            _const_spec((1, B_Q_LORA)), _const_spec((B_CAT, B_Q_LORA)),
            _const_spec((1, B_KV_LORA)), _const_spec((B_KV_LORA, B_CAT)),
            _const_spec((B_OUT, B_KV_LORA)),
        ],
        out_specs=out_specs,
        out_shape=out_shape,
        compiler_params=_params(("parallel", "parallel")),
        name="inproj",
    )(x1, g, w1, wt, tab_a, tab_b, qn, wuqt, kvn, wk, wvt)


def _attn_a_kernel(sink_ref, qt_ref, kp_ref, kc_ref, kn_ref, vp_ref, vc_ref, vn_ref, o_ref, *s_bufs, seq):
    i = pl.program_id(1)
    depth = len(s_bufs)
    row0 = pl.multiple_of(jnp.minimum(i, 0), 3 * BLK)
    k_all = jnp.concatenate([kp_ref[0], kc_ref[0], kn_ref[0]], axis=0)
    v_all = [jnp.concatenate([vp_ref[0, g], vc_ref[0, g], vn_ref[0, g]], axis=1)
             for g in range(A_KV_HEADS)]

    per = TA // BLK
    width = A_GROUP * BLK
    c = lax.broadcasted_iota(jnp.int32, (3 * BLK, width), 0)
    r = lax.broadcasted_iota(jnp.int32, (3 * BLK, width), 1) & (BLK - 1)
    bias_mid = jnp.where(jnp.abs(c - BLK - r) <= WINDOW, 0.0, NEG)
    bias_first = jnp.where(i * TA - BLK + c >= 0, bias_mid, NEG)
    bias_last = jnp.where(i * TA + (per - 2) * BLK + c < seq, bias_mid, NEG)
    biases = [bias_first] + [bias_mid] * (per - 2) + [bias_last]

    items = [(j, g) for j in range(per) for g in range(A_KV_HEADS)]

    def produce(n):
        j, g = items[n]
        cols = slice(j * BLK, (j + 1) * BLK)
        qt = jnp.concatenate([qt_ref[0, g * A_GROUP + hh, :, cols] for hh in range(A_GROUP)], axis=1)
        s = jnp.dot(k_all[j * BLK:(j + 3) * BLK], qt, preferred_element_type=F32) + biases[j]
        s_bufs[n % depth][...] = s
        return jnp.max(s, axis=0, keepdims=True)

    def consume(n, col_max):
        j, g = items[n]
        cols = slice(j * BLK, (j + 1) * BLK)
        sink = sink_ref[g]
        m = jnp.maximum(col_max, sink)
        p = jnp.exp2(s_bufs[n % depth][pl.ds(row0, 3 * BLK), :] - m).astype(BF16)
        pv = jnp.dot(v_all[g][:, j * BLK:(j + 3) * BLK], p, preferred_element_type=F32)
        denom = pv[A_HEAD_DIM:A_HEAD_DIM + 1, :] + jnp.exp2(sink - m)
        o = (pv[0:A_HEAD_DIM, :] / denom).astype(BF16)
        for hh in range(A_GROUP):
            o_ref[0, g * A_GROUP + hh, :, cols] = o[:, hh * BLK:(hh + 1) * BLK]

    col_max = {n: produce(n) for n in range(min(depth - 1, len(items)))}
    for n in range(len(items)):
        ahead = n + depth - 1
        if ahead < len(items):
            col_max[ahead] = produce(ahead)
        consume(n, col_max.pop(n))


def _attn_a(qat, ka, vat, sink_lanes):
    B, _, _, S = qat.shape
    nb = S // BLK
    per = TA // BLK
    prev_i = lambda i: jnp.maximum(i * per - 1, 0)
    next_i = lambda i: jnp.minimum((i + 1) * per, nb - 1)
    k_prev = pl.BlockSpec((1, BLK, A_KV), lambda b, i: (b, prev_i(i), 0))
    k_cur = pl.BlockSpec((1, TA, A_KV), lambda b, i: (b, i, 0))
    k_next = pl.BlockSpec((1, BLK, A_KV), lambda b, i: (b, next_i(i), 0))
    v_prev = pl.BlockSpec((1, A_KV_HEADS, V_EXT, BLK), lambda b, i: (b, 0, 0, prev_i(i)))
    v_cur = pl.BlockSpec((1, A_KV_HEADS, V_EXT, TA), lambda b, i: (b, 0, 0, i))
    v_next = pl.BlockSpec((1, A_KV_HEADS, V_EXT, BLK), lambda b, i: (b, 0, 0, next_i(i)))
    return pl.pallas_call(
        functools.partial(_attn_a_kernel, seq=S),
        grid=(B, S // TA),
        in_specs=[
            _const_spec((A_KV_HEADS, 1, A_GROUP * BLK)),
            pl.BlockSpec((1, A_HEADS, LANES, TA), lambda b, i: (b, 0, 0, i)),
            k_prev, k_cur, k_next, v_prev, v_cur, v_next,
        ],
        out_specs=pl.BlockSpec((1, A_HEADS, A_HEAD_DIM, TA), lambda b, i: (b, 0, 0, i)),
        out_shape=jax.ShapeDtypeStruct((B, A_HEADS, A_HEAD_DIM, S), BF16),
        scratch_shapes=[pltpu.VMEM((3 * BLK, A_GROUP * BLK), F32)] * PIPE_DEPTH,
        compiler_params=_params(("parallel", "parallel")),
        name="attn_a",
    )(sink_lanes, qat, ka, ka, ka, vat, vat, vat)


def _mla_kernel(qt_ref, k_ref, vt_ref, o_ref, m_ref, acc_ref, *s_bufs, n_kv):
    j = pl.program_id(2)
    items = [(c, h) for c in range(TK // TKC) for h in range(B_HEADS)]
    depth = len(s_bufs)
    row0 = pl.multiple_of(jnp.minimum(j, 0), TKC)

    @pl.when(j == 0)
    def _():
        m_ref[...] = jnp.full(m_ref.shape, NEG, F32)
        acc_ref[...] = jnp.zeros(acc_ref.shape, F32)

    def produce(n):
        c, h = items[n]
        s = jnp.dot(k_ref[0, h, c * TKC:(c + 1) * TKC, :], qt_ref[0, h],
                    preferred_element_type=F32)
        s_bufs[n % depth][...] = s
        return jnp.max(s, axis=0, keepdims=True)

    def consume(n, col_max):
        c, h = items[n]
        m_old = m_ref[h]
        m_new = jnp.maximum(m_old, col_max)
        p = jnp.exp2(s_bufs[n % depth][pl.ds(row0, TKC), :] - m_new).astype(BF16)
        pv = jnp.dot(vt_ref[0, c, h], p, preferred_element_type=F32)
        acc_ref[h] = jnp.exp2(m_old - m_new) * acc_ref[h] + pv
        m_ref[h] = m_new

    col_max = {n: produce(n) for n in range(min(depth - 1, len(items)))}
    for n in range(len(items)):
        ahead = n + depth - 1
        if ahead < len(items):
            col_max[ahead] = produce(ahead)
        consume(n, col_max.pop(n))

    @pl.when(j == n_kv - 1)
    def _():
        for h in range(B_HEADS):
            o_ref[0, h] = (acc_ref[h, 0:B_V, :] / acc_ref[h, B_V:B_V + 1, :]).astype(BF16)


def _mla(qt, kcat, vt):
    B, _, _, S = qt.shape
    return pl.pallas_call(
        functools.partial(_mla_kernel, n_kv=S // TK),
        grid=(B, S // TQ, S // TK),
        in_specs=[
            pl.BlockSpec((1, B_HEADS, B_SLAB, TQ), lambda b, i, j: (b, 0, 0, i)),
            pl.BlockSpec((1, B_HEADS, TK, B_SLAB), lambda b, i, j: (b, 0, j, 0)),
            pl.BlockSpec((1, TK // TKC, B_HEADS, V_EXT, TKC), lambda b, i, j: (b, j, 0, 0, 0)),
        ],
        out_specs=pl.BlockSpec((1, B_HEADS, B_V, TQ), lambda b, i, j: (b, 0, 0, i)),
        out_shape=jax.ShapeDtypeStruct((B, B_HEADS, B_V, S), BF16),
        scratch_shapes=[
            pltpu.VMEM((B_HEADS, 1, TQ), F32),
            pltpu.VMEM((B_HEADS, V_EXT, TQ), F32),
        ] + [pltpu.VMEM((TKC, TQ), F32)] * PIPE_DEPTH,
        compiler_params=_params(("parallel", "parallel", "arbitrary")),
        name="mla",
    )(qt, kcat, vt)


def _memkv_kernel(mem_ref, g_ref, w_ref, k_ref, v_ref):
    mn = _rms(mem_ref[0], g_ref[...]).astype(BF16)
    kv = jnp.dot(mn, w_ref[...], preferred_element_type=F32)
    k_ref[0] = kv[:, :C_Q].astype(BF16)
    v_ref[0] = kv[:, C_Q:].astype(BF16)


def _memkv(mem, g, w):
    B = mem.shape[0]
    out = jax.ShapeDtypeStruct((B, N_MEM, C_Q), BF16)
    spec = pl.BlockSpec((1, N_MEM, C_Q), lambda b: (b, 0, 0))
    return pl.pallas_call(
        _memkv_kernel,
        grid=(B,),
        in_specs=[pl.BlockSpec((1, N_MEM, D_MODEL), lambda b: (b, 0, 0)),
                  _const_spec((1, D_MODEL)), _const_spec((D_MODEL, 2 * C_Q))],
        out_specs=(spec, spec),
        out_shape=(out, out),
        compiler_params=_params(("parallel",)),
        name="memkv",
    )(mem, g, w)


def _merge_tile(x_ref, oat_ref, obt_ref, qc_ref, mk_ref, mv_ref, g_ref, wgl_ref, wa_ref, wb_ref, wc_ref,
                  wout_ref, o_ref):
    x = x_ref[0]
    tm = x.shape[0]
    u = _rms(x, g_ref[...]).astype(BF16)

    oc = []
    for h in range(C_HEADS):
        sl = slice(h * C_HEAD_DIM, (h + 1) * C_HEAD_DIM)
        s = lax.dot_general(qc_ref[0, :, sl], mk_ref[0, :, sl], NT_DIMS,
                            preferred_element_type=F32) * (C_HEAD_DIM ** -0.5)
        e = jnp.exp(s - jnp.max(s, axis=-1, keepdims=True))
        pv = jnp.dot(e.astype(BF16), mv_ref[0, :, sl], preferred_element_type=F32)
        oc.append((pv / jnp.sum(e, axis=-1, keepdims=True)).astype(BF16))
    oc = jnp.concatenate(oc, axis=1)

    branches = (
        lax.dot_general(oat_ref[0].reshape(A_Q, tm), wa_ref[...], TN_DIMS, preferred_element_type=F32),
        lax.dot_general(obt_ref[0].reshape(B_OUT, tm), wb_ref[...], TN_DIMS, preferred_element_type=F32),
        jnp.dot(oc, wc_ref[...], preferred_element_type=F32),
    )
    merged = jnp.zeros(x.shape, F32)
    for i, br in enumerate(branches):
        gl = jnp.dot(u, wgl_ref[:, i * D_MODEL:(i + 1) * D_MODEL], preferred_element_type=F32)
        merged = merged + br * (1.0 / (1.0 + jnp.exp(-gl)))
    o_ref[0] = x + jnp.dot(merged.astype(BF16), wout_ref[...], preferred_element_type=F32)


def _merge(x1, oat, obt, qc, mk, mv, g, wgl, wa, wb, wc, wout):
    B, S, D = x1.shape
    tok = lambda w: pl.BlockSpec((1, TM_MERGE, w), lambda b, i: (b, i, 0))
    head_t = lambda heads, rows: pl.BlockSpec((1, heads, rows, TM_MERGE), lambda b, i: (b, 0, 0, i))
    mem = pl.BlockSpec((1, N_MEM, C_Q), lambda b, i: (b, 0, 0))
    return pl.pallas_call(
        _sub_tiled(_merge_tile, [(1, 1), (3, 1), (3, 1), (1, 1)] + [None] * 8 + [(1, 1)]),
        grid=(B, S // TM_MERGE),
        in_specs=[
            tok(D), head_t(A_HEADS, A_HEAD_DIM), head_t(B_HEADS, B_V), tok(C_Q), mem, mem,
            _const_spec((1, D)), _const_spec((D, N_BRANCH * D)),
            _const_spec((A_Q, D)), _const_spec((B_OUT, D)), _const_spec((C_Q, D)),
            _const_spec((D, D)),
        ],
        out_specs=tok(D),
        out_shape=jax.ShapeDtypeStruct((B, S, D), F32),
        compiler_params=_params(("parallel", "parallel")),
        name="merge",
    )(x1, oat, obt, qc, mk, mv, g, wgl, wa, wb, wc, wout)


def _rope_tables(seq, half):
    inv = ROPE_THETA ** (-jnp.arange(0, 2 * half, 2, dtype=F32) / (2 * half))
    ang = inv[:, None] * jnp.arange(seq, dtype=F32)[None, :]
    return jnp.stack([jnp.cos(ang), jnp.sin(ang)])


def _prepare(ffn1_norm, ffn1_w_in, ffn1_w_out, mix_norm, w_in, mla_q_norm, mla_w_uq, mla_kv_norm,
             mla_w_ukv, attn_sink, mem_norm, w_mem_kv, w_branch_a, w_branch_b, w_branch_c, w_out,
             ffn2_norm, ffn2_w_in, ffn2_w_out, final_norm):
    row = lambda v: v.reshape(1, -1).astype(F32)
    cols, o = [], 0
    for s in IN_SIZES:
        cols.append(w_in[0][:, o:o + s])
        o += s
    w_qa, w_ka, w_va, w_cq, w_ckv, w_kr, w_qc, w_gl = cols
    w1 = jnp.concatenate([w_cq, w_ckv, w_qc], axis=1).astype(BF16)
    wt = jnp.concatenate([w_qa, w_va, w_ka, w_kr], axis=1).T.astype(BF16)
    wuq = mla_w_uq[0].reshape(B_Q_LORA, B_HEADS, B_NOPE + B_ROPE)
    wuqt = jnp.pad(wuq, ((0, 0), (0, 0), (0, B_SLAB - B_NOPE - B_ROPE))).reshape(B_Q_LORA, B_CAT).T.astype(BF16)
    wukv = mla_w_ukv[0].reshape(B_KV_LORA, B_HEADS, B_NOPE + B_V)
    wk = jnp.pad(wukv[:, :, :B_NOPE], ((0, 0), (0, 0), (0, B_SLAB - B_NOPE))).reshape(B_KV_LORA, B_CAT).astype(BF16)
    wvt = wukv[:, :, B_NOPE:].reshape(B_KV_LORA, B_OUT).T.astype(BF16)
    sink_lanes = jnp.repeat(attn_sink[0].astype(F32) * LOG2E, BLK).reshape(A_KV_HEADS, 1, A_GROUP * BLK)
    return dict(
        ffn1=(row(ffn1_norm[0]), ffn1_w_in[0].astype(BF16), ffn1_w_out[0].astype(BF16)),
        ffn2=(row(ffn2_norm[0]), ffn2_w_in[0].astype(BF16), ffn2_w_out[0].astype(BF16)),
        final=row(final_norm),
        inproj=(row(mix_norm[0]), w1, wt),
        mla=(row(mla_q_norm[0]), wuqt, row(mla_kv_norm[0]), wk, wvt),
        sink=sink_lanes,
        mem=(row(mem_norm[0]), w_mem_kv[0].astype(BF16)),
        merge=(row(mix_norm[0]), w_gl.astype(BF16), w_branch_a[0].astype(BF16), w_branch_b[0].astype(BF16),
               w_branch_c[0].astype(BF16), w_out[0].astype(BF16)),
    )


def _trunk(x, mem, w, tables):
    x1 = _ffn(x, *w["ffn1"], w["final"], final_norm=False)
    qat, ka, vat, qt, kcat, vt, qc = _inproj(x1, *w["inproj"], *tables, *w["mla"])
    oat = _attn_a(qat, ka, vat, w["sink"])
    obt = _mla(qt, kcat, vt)
    mk, mv = _memkv(mem, *w["mem"])
    x2 = _merge(x1, oat, obt, qc, mk, mv, *w["merge"])
    return _ffn(x2, *w["ffn2"], w["final"], final_norm=True)


def kernel(x_prompt, x_sample, mem_prompt, mem_sample, ffn1_norm, ffn1_w_in, ffn1_w_out, mix_norm, w_in,
           mla_q_norm, mla_w_uq, mla_kv_norm, mla_w_ukv, attn_sink, mem_norm, w_mem_kv, w_branch_a,
           w_branch_b, w_branch_c, w_out, ffn2_norm, ffn2_w_in, ffn2_w_out, final_norm):
    w = _prepare(ffn1_norm, ffn1_w_in, ffn1_w_out, mix_norm, w_in, mla_q_norm, mla_w_uq, mla_kv_norm,
                 mla_w_ukv, attn_sink, mem_norm, w_mem_kv, w_branch_a, w_branch_b, w_branch_c, w_out,
                 ffn2_norm, ffn2_w_in, ffn2_w_out, final_norm)
    seq = max(x_prompt.shape[1], x_sample.shape[1])
    tables = (_rope_tables(seq, A_HEAD_DIM // 2), _rope_tables(seq, B_ROPE // 2))
    return (_trunk(x_prompt, mem_prompt, w, tables), _trunk(x_sample, mem_sample, w, tables))
```

```python
import functools

import jax
import jax.numpy as jnp
from jax import lax
from jax.experimental import pallas as pl
from jax.experimental.pallas import tpu as pltpu

D_MODEL = 1024
N_MEM = 256
BLK = 128
WINDOW = 128
ROPE_THETA = 10000.0
EPS = 1e-6
NEG = -1e30
A_HEADS = 8
A_KV_HEADS = 2
A_GROUP = A_HEADS // A_KV_HEADS
A_HEAD_DIM = 64
A_Q = A_HEADS * A_HEAD_DIM
A_KV = A_KV_HEADS * A_HEAD_DIM
B_HEADS = 8
B_Q_LORA = 384
B_KV_LORA = 256
B_NOPE = 64
B_ROPE = 32
B_V = 64
B_OUT = B_HEADS * B_V
C_HEADS = 4
C_HEAD_DIM = 128
C_Q = C_HEADS * C_HEAD_DIM
N_BRANCH = 3
D_FF = 2816
IN_SIZES = (A_Q, A_KV, A_KV, B_Q_LORA, B_KV_LORA, B_ROPE, C_Q, N_BRANCH * D_MODEL)

LANES = 128
B_SLAB = LANES
B_CAT = B_HEADS * B_SLAB
OFF_CQ, OFF_CKV, OFF_QC = 0, 384, 640
W1_COLS = 1152
ROW_QA, ROW_VA, ROW_KA, ROW_KR = 0, 512, 640, 768
WT_ROWS = 800

SUB = 512
TM_FFN = 1024
FF_CHUNK = 512
TM_IN = 1024
TA = 1024
TQ = 512
TK = 2048
TKC = 512
PIPE_DEPTH = 3
V_EXT = 80
TM_MERGE = 1024
VMEM_LIMIT = 52 * 1024 * 1024
LOG2E = 1.4426950408889634

BF16 = jnp.bfloat16
F32 = jnp.float32
NT_DIMS = (((1,), (1,)), ((), ()))
TN_DIMS = (((0,), (0,)), ((), ()))


def _rms(x, g):
    return x * lax.rsqrt(jnp.mean(x * x, axis=-1, keepdims=True) + EPS) * g


def _const_spec(shape):
    nd = len(shape)
    return pl.BlockSpec(shape, lambda *_: (0,) * nd, pipeline_mode=pl.Buffered(1))


def _params(sem):
    return pltpu.CompilerParams(dimension_semantics=sem, vmem_limit_bytes=VMEM_LIMIT)


def _ones_row_tile(width):
    return (lax.broadcasted_iota(jnp.int32, (V_EXT - B_V, width), 0) == 0).astype(BF16)


def _sub_tiled(tile_fn, token_axes):
    def kernel(*refs):
        n_sub = refs[0].shape[token_axes[0][0]] * token_axes[0][1] // SUB
        for t in range(n_sub):
            views = []
            for ref, ax in zip(refs, token_axes):
                if ax is None:
                    views.append(ref)
                    continue
                axis, granule = ax
                idx = [slice(None)] * len(ref.shape)
                idx[axis] = pl.ds(t * SUB // granule, SUB // granule)
                views.append(ref.at[tuple(idx)])
            tile_fn(*views)
    return kernel


def _ffn_tile(x_ref, g_ref, w1_ref, w2_ref, fn_ref, o_ref, *, final_norm):
    x = x_ref[0]
    xn = _rms(x, g_ref[...]).astype(BF16)
    acc = jnp.zeros(x.shape, F32)
    lo = 0
    while lo < D_FF:
        fc = min(FF_CHUNK, D_FF - lo)
        g = jnp.dot(xn, w1_ref[:, lo:lo + fc], preferred_element_type=F32)
        u = jnp.dot(xn, w1_ref[:, D_FF + lo:D_FF + lo + fc], preferred_element_type=F32)
        h = (g * (1.0 / (1.0 + jnp.exp(-g))) * u).astype(BF16)
        acc = acc + jnp.dot(h, w2_ref[lo:lo + fc, :], preferred_element_type=F32)
        lo += fc
    y = x + 0.5 * acc
    if final_norm:
        y = _rms(y, fn_ref[...])
    o_ref[0] = y


def _ffn(x, g, w1, w2, fn, final_norm):
    B, S, D = x.shape
    return pl.pallas_call(
        _sub_tiled(functools.partial(_ffn_tile, final_norm=final_norm),
                   [(1, 1), None, None, None, None, (1, 1)]),
        grid=(B, S // TM_FFN),
        in_specs=[
            pl.BlockSpec((1, TM_FFN, D), lambda b, i: (b, i, 0)),
            _const_spec((1, D)),
            _const_spec((D, 2 * D_FF)),
            _const_spec((D_FF, D)),
            _const_spec((1, D)),
        ],
        out_specs=pl.BlockSpec((1, TM_FFN, D), lambda b, i: (b, i, 0)),
        out_shape=jax.ShapeDtypeStruct((B, S, D), F32),
        compiler_params=_params(("parallel", "parallel")),
        name="ffn_final" if final_norm else "ffn",
    )(x, g, w1, w2, fn)


def _rope_rows(x, tab_ref):
    half = x.shape[0] // 2
    x1, x2 = x[:half], x[half:]
    cos, sin = tab_ref[0], tab_ref[1]
    return jnp.concatenate([x1 * cos - x2 * sin, x2 * cos + x1 * sin], axis=0)


def _inproj_tile(x_ref, g_ref, w1_ref, wt_ref, ta_ref, tb_ref, qn_ref, wuqt_ref, kvn_ref,
                   wk_ref, wvt_ref, qat_ref, ka_ref, vat_ref, qt_ref, kcat_ref, vt_ref, qc_ref):
    tm = x_ref.shape[1]
    u = _rms(x_ref[0], g_ref[...]).astype(BF16)
    z = jnp.dot(u, w1_ref[...], preferred_element_type=F32)
    zt = lax.dot_general(wt_ref[...], u, NT_DIMS, preferred_element_type=F32)

    scale_a = A_HEAD_DIM ** -0.5 * LOG2E
    zeros_half = jnp.zeros((A_HEAD_DIM, tm), BF16)
    for h in range(A_HEADS):
        rows = slice(ROW_QA + h * A_HEAD_DIM, ROW_QA + (h + 1) * A_HEAD_DIM)
        q = (_rope_rows(zt[rows, :], ta_ref) * scale_a).astype(BF16)
        g_kv = h // A_GROUP
        qat_ref[0, h, g_kv * A_HEAD_DIM:(g_kv + 1) * A_HEAD_DIM, :] = q
        qat_ref[0, h, (1 - g_kv) * A_HEAD_DIM:(2 - g_kv) * A_HEAD_DIM, :] = zeros_half
    ka_t = jnp.concatenate(
        [_rope_rows(zt[ROW_KA + g * A_HEAD_DIM:ROW_KA + (g + 1) * A_HEAD_DIM, :], ta_ref)
         for g in range(A_KV_HEADS)], axis=0)
    ka_ref[0] = ka_t.T.astype(BF16)
    ones_a = _ones_row_tile(tm)
    for g_kv in range(A_KV_HEADS):
        rows = slice(ROW_VA + g_kv * A_HEAD_DIM, ROW_VA + (g_kv + 1) * A_HEAD_DIM)
        vat_ref[0, g_kv, 0:A_HEAD_DIM, :] = zt[rows, :].astype(BF16)
        vat_ref[0, g_kv, A_HEAD_DIM:V_EXT, :] = ones_a

    qc_ref[0] = z[:, OFF_QC:OFF_QC + C_Q].astype(BF16)

    cqn = _rms(z[:, OFF_CQ:OFF_CQ + B_Q_LORA], qn_ref[...]).astype(BF16)
    qt = lax.dot_general(wuqt_ref[...], cqn, NT_DIMS, preferred_element_type=F32)
    scale_b = (B_NOPE + B_ROPE) ** -0.5 * LOG2E
    rope_lo, rope_hi = B_NOPE, B_NOPE + B_ROPE
    for h in range(B_HEADS):
        slab = qt[h * B_SLAB:(h + 1) * B_SLAB, :]
        slab = jnp.concatenate([slab[:rope_lo], _rope_rows(slab[rope_lo:rope_hi], tb_ref), slab[rope_hi:]], axis=0)
        qt_ref[0, h] = (slab * scale_b).astype(BF16)

    cn = _rms(z[:, OFF_CKV:OFF_CKV + B_KV_LORA], kvn_ref[...]).astype(BF16)
    kn = jnp.dot(cn, wk_ref[...], preferred_element_type=F32)
    kr_t = jnp.concatenate([jnp.zeros((rope_lo, tm), F32),
                            _rope_rows(zt[ROW_KR:ROW_KR + B_ROPE, :], tb_ref),
                            jnp.zeros((B_SLAB - rope_hi, tm), F32)], axis=0)
    kr = kr_t.T
    for h in range(B_HEADS):
        kcat_ref[0, h] = (kn[:, h * B_SLAB:(h + 1) * B_SLAB] + kr).astype(BF16)

    vt = lax.dot_general(wvt_ref[...], cn, NT_DIMS, preferred_element_type=F32)
    ones_b = _ones_row_tile(TKC)
    for c in range(tm // TKC):
        for h in range(B_HEADS):
            vt_ref[0, c, h, 0:B_V, :] = vt[h * B_V:(h + 1) * B_V, c * TKC:(c + 1) * TKC].astype(BF16)
            vt_ref[0, c, h, B_V:V_EXT, :] = ones_b


def _inproj(x1, g, w1, wt, tab_a, tab_b, qn, wuqt, kvn, wk, wvt):
    B, S, D = x1.shape
    tok = lambda w: pl.BlockSpec((1, TM_IN, w), lambda b, i: (b, i, 0))
    tab = lambda half: pl.BlockSpec((2, half, TM_IN), lambda b, i: (0, 0, i))
    head_t = lambda heads, rows: pl.BlockSpec((1, heads, rows, TM_IN), lambda b, i: (b, 0, 0, i))
    out_shape = (
        jax.ShapeDtypeStruct((B, A_HEADS, LANES, S), BF16),
        jax.ShapeDtypeStruct((B, S, A_KV), BF16),
        jax.ShapeDtypeStruct((B, A_KV_HEADS, V_EXT, S), BF16),
        jax.ShapeDtypeStruct((B, B_HEADS, B_SLAB, S), BF16),
        jax.ShapeDtypeStruct((B, B_HEADS, S, B_SLAB), BF16),
        jax.ShapeDtypeStruct((B, S // TKC, B_HEADS, V_EXT, TKC), BF16),
        jax.ShapeDtypeStruct((B, S, C_Q), BF16),
    )
    out_specs = (
        head_t(A_HEADS, LANES), tok(A_KV), head_t(A_KV_HEADS, V_EXT), head_t(B_HEADS, B_SLAB),
        pl.BlockSpec((1, B_HEADS, TM_IN, B_SLAB), lambda b, i: (b, 0, i, 0)),
        pl.BlockSpec((1, TM_IN // TKC, B_HEADS, V_EXT, TKC), lambda b, i: (b, i, 0, 0, 0)),
        tok(C_Q),
    )
    return pl.pallas_call(
        _sub_tiled(_inproj_tile, [(1, 1), None, None, None, (2, 1), (2, 1), None, None, None, None, None,
                                  (3, 1), (1, 1), (3, 1), (3, 1), (2, 1), (1, TKC), (1, 1)]),
        grid=(B, S // TM_IN),
        in_specs=[
            tok(D), _const_spec((1, D)), _const_spec((D, W1_COLS)), _const_spec((WT_ROWS, D)),
            tab(A_HEAD_DIM // 2), tab(B_ROPE // 2),
            _const_spec((1, B_Q_LORA)), _const_spec((B_CAT, B_Q_LORA)),
            _const_spec((1, B_KV_LORA)), _const_spec((B_KV_LORA, B_CAT)),
            _const_spec((B_OUT, B_KV_LORA)),
        ],
        out_specs=out_specs,
        out_shape=out_shape,
        compiler_params=_params(("parallel", "parallel")),
        name="inproj",
    )(x1, g, w1, wt, tab_a, tab_b, qn, wuqt, kvn, wk, wvt)


def _attn_a_kernel(sink_ref, qt_ref, kp_ref, kc_ref, kn_ref, vp_ref, vc_ref, vn_ref, o_ref, *s_bufs, seq):
    i = pl.program_id(1)
    depth = len(s_bufs)
    row0 = pl.multiple_of(jnp.minimum(i, 0), 3 * BLK)
    k_all = jnp.concatenate([kp_ref[0], kc_ref[0], kn_ref[0]], axis=0)
    v_all = [jnp.concatenate([vp_ref[0, g], vc_ref[0, g], vn_ref[0, g]], axis=1)
             for g in range(A_KV_HEADS)]

    per = TA // BLK
    width = A_GROUP * BLK
    c = lax.broadcasted_iota(jnp.int32, (3 * BLK, width), 0)
    r = lax.broadcasted_iota(jnp.int32, (3 * BLK, width), 1) & (BLK - 1)
    bias_mid = jnp.where(jnp.abs(c - BLK - r) <= WINDOW, 0.0, NEG)
    bias_first = jnp.where(i * TA - BLK + c >= 0, bias_mid, NEG)
    bias_last = jnp.where(i * TA + (per - 2) * BLK + c < seq, bias_mid, NEG)
    biases = [bias_first] + [bias_mid] * (per - 2) + [bias_last]

    items = [(j, g) for j in range(per) for g in range(A_KV_HEADS)]

    def produce(n):
        j, g = items[n]
        cols = slice(j * BLK, (j + 1) * BLK)
        qt = jnp.concatenate([qt_ref[0, g * A_GROUP + hh, :, cols] for hh in range(A_GROUP)], axis=1)
        s = jnp.dot(k_all[j * BLK:(j + 3) * BLK], qt, preferred_element_type=F32) + biases[j]
        s_bufs[n % depth][...] = s
        return jnp.max(s, axis=0, keepdims=True)

    def consume(n, col_max):
        j, g = items[n]
        cols = slice(j * BLK, (j + 1) * BLK)
        sink = sink_ref[g]
        m = jnp.maximum(col_max, sink)
        p = jnp.exp2(s_bufs[n % depth][pl.ds(row0, 3 * BLK), :] - m).astype(BF16)
        pv = jnp.dot(v_all[g][:, j * BLK:(j + 3) * BLK], p, preferred_element_type=F32)
        denom = pv[A_HEAD_DIM:A_HEAD_DIM + 1, :] + jnp.exp2(sink - m)
        o = (pv[0:A_HEAD_DIM, :] / denom).astype(BF16)
        for hh in range(A_GROUP):
            o_ref[0, g * A_GROUP + hh, :, cols] = o[:, hh * BLK:(hh + 1) * BLK]

    col_max = {n: produce(n) for n in range(min(depth - 1, len(items)))}
    for n in range(len(items)):
        ahead = n + depth - 1
        if ahead < len(items):
            col_max[ahead] = produce(ahead)
        consume(n, col_max.pop(n))


def _attn_a(qat, ka, vat, sink_lanes):
    B, _, _, S = qat.shape
    nb = S // BLK
    per = TA // BLK
    prev_i = lambda i: jnp.maximum(i * per - 1, 0)
    next_i = lambda i: jnp.minimum((i + 1) * per, nb - 1)
    k_prev = pl.BlockSpec((1, BLK, A_KV), lambda b, i: (b, prev_i(i), 0))
    k_cur = pl.BlockSpec((1, TA, A_KV), lambda b, i: (b, i, 0))
    k_next = pl.BlockSpec((1, BLK, A_KV), lambda b, i: (b, next_i(i), 0))
    v_prev = pl.BlockSpec((1, A_KV_HEADS, V_EXT, BLK), lambda b, i: (b, 0, 0, prev_i(i)))
    v_cur = pl.BlockSpec((1, A_KV_HEADS, V_EXT, TA), lambda b, i: (b, 0, 0, i))
    v_next = pl.BlockSpec((1, A_KV_HEADS, V_EXT, BLK), lambda b, i: (b, 0, 0, next_i(i)))
    return pl.pallas_call(
        functools.partial(_attn_a_kernel, seq=S),
        grid=(B, S // TA),
        in_specs=[
            _const_spec((A_KV_HEADS, 1, A_GROUP * BLK)),
            pl.BlockSpec((1, A_HEADS, LANES, TA), lambda b, i: (b, 0, 0, i)),
            k_prev, k_cur, k_next, v_prev, v_cur, v_next,
        ],
        out_specs=pl.BlockSpec((1, A_HEADS, A_HEAD_DIM, TA), lambda b, i: (b, 0, 0, i)),
        out_shape=jax.ShapeDtypeStruct((B, A_HEADS, A_HEAD_DIM, S), BF16),
        scratch_shapes=[pltpu.VMEM((3 * BLK, A_GROUP * BLK), F32)] * PIPE_DEPTH,
        compiler_params=_params(("parallel", "parallel")),
        name="attn_a",
    )(sink_lanes, qat, ka, ka, ka, vat, vat, vat)


def _mla_kernel(qt_ref, k_ref, vt_ref, o_ref, m_ref, acc_ref, *s_bufs, n_kv):
    j = pl.program_id(2)
    items = [(c, h) for c in range(TK // TKC) for h in range(B_HEADS)]
    depth = len(s_bufs)
    row0 = pl.multiple_of(jnp.minimum(j, 0), TKC)

    @pl.when(j == 0)
    def _():
        m_ref[...] = jnp.full(m_ref.shape, NEG, F32)
        acc_ref[...] = jnp.zeros(acc_ref.shape, F32)

    def produce(n):
        c, h = items[n]
        s = jnp.dot(k_ref[0, h, c * TKC:(c + 1) * TKC, :], qt_ref[0, h],
                    preferred_element_type=F32)
        s_bufs[n % depth][...] = s
        return jnp.max(s, axis=0, keepdims=True)

    def consume(n, col_max):
        c, h = items[n]
        m_old = m_ref[h]
        m_new = jnp.maximum(m_old, col_max)
        p = jnp.exp2(s_bufs[n % depth][pl.ds(row0, TKC), :] - m_new).astype(BF16)
        pv = jnp.dot(vt_ref[0, c, h], p, preferred_element_type=F32)
        acc_ref[h] = jnp.exp2(m_old - m_new) * acc_ref[h] + pv
        m_ref[h] = m_new

    col_max = {n: produce(n) for n in range(min(depth - 1, len(items)))}
    for n in range(len(items)):
        ahead = n + depth - 1
        if ahead < len(items):
            col_max[ahead] = produce(ahead)
        consume(n, col_max.pop(n))

    @pl.when(j == n_kv - 1)
    def _():
        for h in range(B_HEADS):
            o_ref[0, h] = (acc_ref[h, 0:B_V, :] / acc_ref[h, B_V:B_V + 1, :]).astype(BF16)


def _mla(qt, kcat, vt):
    B, _, _, S = qt.shape
    return pl.pallas_call(
        functools.partial(_mla_kernel, n_kv=S // TK),
        grid=(B, S // TQ, S // TK),
        in_specs=[
            pl.BlockSpec((1, B_HEADS, B_SLAB, TQ), lambda b, i, j: (b, 0, 0, i)),
            pl.BlockSpec((1, B_HEADS, TK, B_SLAB), lambda b, i, j: (b, 0, j, 0)),
            pl.BlockSpec((1, TK // TKC, B_HEADS, V_EXT, TKC), lambda b, i, j: (b, j, 0, 0, 0)),
        ],
        out_specs=pl.BlockSpec((1, B_HEADS, B_V, TQ), lambda b, i, j: (b, 0, 0, i)),
        out_shape=jax.ShapeDtypeStruct((B, B_HEADS, B_V, S), BF16),
        scratch_shapes=[
            pltpu.VMEM((B_HEADS, 1, TQ), F32),
            pltpu.VMEM((B_HEADS, V_EXT, TQ), F32),
        ] + [pltpu.VMEM((TKC, TQ), F32)] * PIPE_DEPTH,
        compiler_params=_params(("parallel", "parallel", "arbitrary")),
        name="mla",
    )(qt, kcat, vt)


def _memkv_kernel(mem_ref, g_ref, w_ref, k_ref, v_ref):
    mn = _rms(mem_ref[0], g_ref[...]).astype(BF16)
    kv = jnp.dot(mn, w_ref[...], preferred_element_type=F32)
    k_ref[0] = kv[:, :C_Q].astype(BF16)
    v_ref[0] = kv[:, C_Q:].astype(BF16)


def _memkv(mem, g, w):
    B = mem.shape[0]
    out = jax.ShapeDtypeStruct((B, N_MEM, C_Q), BF16)
    spec = pl.BlockSpec((1, N_MEM, C_Q), lambda b: (b, 0, 0))
    return pl.pallas_call(
        _memkv_kernel,
        grid=(B,),
        in_specs=[pl.BlockSpec((1, N_MEM, D_MODEL), lambda b: (b, 0, 0)),
                  _const_spec((1, D_MODEL)), _const_spec((D_MODEL, 2 * C_Q))],
        out_specs=(spec, spec),
        out_shape=(out, out),
        compiler_params=_params(("parallel",)),
        name="memkv",
    )(mem, g, w)


def _merge_tile(x_ref, oat_ref, obt_ref, qc_ref, mk_ref, mv_ref, g_ref, wgl_ref, wa_ref, wb_ref, wc_ref,
                  wout_ref, o_ref):
    x = x_ref[0]
    tm = x.shape[0]
    u = _rms(x, g_ref[...]).astype(BF16)

    oc = []
    for h in range(C_HEADS):
        sl = slice(h * C_HEAD_DIM, (h + 1) * C_HEAD_DIM)
        s = lax.dot_general(qc_ref[0, :, sl], mk_ref[0, :, sl], NT_DIMS,
                            preferred_element_type=F32) * (C_HEAD_DIM ** -0.5)
        e = jnp.exp(s - jnp.max(s, axis=-1, keepdims=True))
        pv = jnp.dot(e.astype(BF16), mv_ref[0, :, sl], preferred_element_type=F32)
        oc.append((pv / jnp.sum(e, axis=-1, keepdims=True)).astype(BF16))
    oc = jnp.concatenate(oc, axis=1)

    branches = (
        lax.dot_general(oat_ref[0].reshape(A_Q, tm), wa_ref[...], TN_DIMS, preferred_element_type=F32),
        lax.dot_general(obt_ref[0].reshape(B_OUT, tm), wb_ref[...], TN_DIMS, preferred_element_type=F32),
        jnp.dot(oc, wc_ref[...], preferred_element_type=F32),
    )
    merged = jnp.zeros(x.shape, F32)
    for i, br in enumerate(branches):
        gl = jnp.dot(u, wgl_ref[:, i * D_MODEL:(i + 1) * D_MODEL], preferred_element_type=F32)
        merged = merged + br * (1.0 / (1.0 + jnp.exp(-gl)))
    o_ref[0] = x + jnp.dot(merged.astype(BF16), wout_ref[...], preferred_element_type=F32)


def _merge(x1, oat, obt, qc, mk, mv, g, wgl, wa, wb, wc, wout):
    B, S, D = x1.shape
    tok = lambda w: pl.BlockSpec((1, TM_MERGE, w), lambda b, i: (b, i, 0))
    head_t = lambda heads, rows: pl.BlockSpec((1, heads, rows, TM_MERGE), lambda b, i: (b, 0, 0, i))
    mem = pl.BlockSpec((1, N_MEM, C_Q), lambda b, i: (b, 0, 0))
    return pl.pallas_call(
        _sub_tiled(_merge_tile, [(1, 1), (3, 1), (3, 1), (1, 1)] + [None] * 8 + [(1, 1)]),
        grid=(B, S // TM_MERGE),
        in_specs=[
            tok(D), head_t(A_HEADS, A_HEAD_DIM), head_t(B_HEADS, B_V), tok(C_Q), mem, mem,
            _const_spec((1, D)), _const_spec((D, N_BRANCH * D)),
            _const_spec((A_Q, D)), _const_spec((B_OUT, D)), _const_spec((C_Q, D)),
            _const_spec((D, D)),
        ],
        out_specs=tok(D),
        out_shape=jax.ShapeDtypeStruct((B, S, D), F32),
        compiler_params=_params(("parallel", "parallel")),
        name="merge",
    )(x1, oat, obt, qc, mk, mv, g, wgl, wa, wb, wc, wout)


def _rope_tables(seq, half):
    inv = ROPE_THETA ** (-jnp.arange(0, 2 * half, 2, dtype=F32) / (2 * half))
    ang = inv[:, None] * jnp.arange(seq, dtype=F32)[None, :]
    return jnp.stack([jnp.cos(ang), jnp.sin(ang)])


def _prepare(ffn1_norm, ffn1_w_in, ffn1_w_out, mix_norm, w_in, mla_q_norm, mla_w_uq, mla_kv_norm,
             mla_w_ukv, attn_sink, mem_norm, w_mem_kv, w_branch_a, w_branch_b, w_branch_c, w_out,
             ffn2_norm, ffn2_w_in, ffn2_w_out, final_norm):
    row = lambda v: v.reshape(1, -1).astype(F32)
    cols, o = [], 0
    for s in IN_SIZES:
        cols.append(w_in[0][:, o:o + s])
        o += s
    w_qa, w_ka, w_va, w_cq, w_ckv, w_kr, w_qc, w_gl = cols
    w1 = jnp.concatenate([w_cq, w_ckv, w_qc], axis=1).astype(BF16)
    wt = jnp.concatenate([w_qa, w_va, w_ka, w_kr], axis=1).T.astype(BF16)
    wuq = mla_w_uq[0].reshape(B_Q_LORA, B_HEADS, B_NOPE + B_ROPE)
    wuqt = jnp.pad(wuq, ((0, 0), (0, 0), (0, B_SLAB - B_NOPE - B_ROPE))).reshape(B_Q_LORA, B_CAT).T.astype(BF16)
    wukv = mla_w_ukv[0].reshape(B_KV_LORA, B_HEADS, B_NOPE + B_V)
    wk = jnp.pad(wukv[:, :, :B_NOPE], ((0, 0), (0, 0), (0, B_SLAB - B_NOPE))).reshape(B_KV_LORA, B_CAT).astype(BF16)
    wvt = wukv[:, :, B_NOPE:].reshape(B_KV_LORA, B_OUT).T.astype(BF16)
    sink_lanes = jnp.repeat(attn_sink[0].astype(F32) * LOG2E, BLK).reshape(A_KV_HEADS, 1, A_GROUP * BLK)
    return dict(
        ffn1=(row(ffn1_norm[0]), ffn1_w_in[0].astype(BF16), ffn1_w_out[0].astype(BF16)),
        ffn2=(row(ffn2_norm[0]), ffn2_w_in[0].astype(BF16), ffn2_w_out[0].astype(BF16)),
        final=row(final_norm),
        inproj=(row(mix_norm[0]), w1, wt),
        mla=(row(mla_q_norm[0]), wuqt, row(mla_kv_norm[0]), wk, wvt),
        sink=sink_lanes,
        mem=(row(mem_norm[0]), w_mem_kv[0].astype(BF16)),
        merge=(row(mix_norm[0]), w_gl.astype(BF16), w_branch_a[0].astype(BF16), w_branch_b[0].astype(BF16),
               w_branch_c[0].astype(BF16), w_out[0].astype(BF16)),
    )


def _trunk(x, mem, w, tables):
    x1 = _ffn(x, *w["ffn1"], w["final"], final_norm=False)
    qat, ka, vat, qt, kcat, vt, qc = _inproj(x1, *w["inproj"], *tables, *w["mla"])
    oat = _attn_a(qat, ka, vat, w["sink"])
    obt = _mla(qt, kcat, vt)
    mk, mv = _memkv(mem, *w["mem"])
    x2 = _merge(x1, oat, obt, qc, mk, mv, *w["merge"])
    return _ffn(x2, *w["ffn2"], w["final"], final_norm=True)


def kernel(x_prompt, x_sample, mem_prompt, mem_sample, ffn1_norm, ffn1_w_in, ffn1_w_out, mix_norm, w_in,
           mla_q_norm, mla_w_uq, mla_kv_norm, mla_w_ukv, attn_sink, mem_norm, w_mem_kv, w_branch_a,
           w_branch_b, w_branch_c, w_out, ffn2_norm, ffn2_w_in, ffn2_w_out, final_norm):
    w = _prepare(ffn1_norm, ffn1_w_in, ffn1_w_out, mix_norm, w_in, mla_q_norm, mla_w_uq, mla_kv_norm,
                 mla_w_ukv, attn_sink, mem_norm, w_mem_kv, w_branch_a, w_branch_b, w_branch_c, w_out,
                 ffn2_norm, ffn2_w_in, ffn2_w_out, final_norm)
    seq = max(x_prompt.shape[1], x_sample.shape[1])
    tables = (_rope_tables(seq, A_HEAD_DIM // 2), _rope_tables(seq, B_ROPE // 2))
    return (_trunk(x_prompt, mem_prompt, w, tables), _trunk(x_sample, mem_sample, w, tables))
```

```python
import functools

import jax
import jax.numpy as jnp
from jax import lax
from jax.experimental import pallas as pl
from jax.experimental.pallas import tpu as pltpu

D_MODEL = 1024
N_MEM = 256
BLK = 128
WINDOW = 128
ROPE_THETA = 10000.0
EPS = 1e-6
NEG = -1e30
A_HEADS = 8
A_KV_HEADS = 2
A_GROUP = A_HEADS // A_KV_HEADS
A_HEAD_DIM = 64
A_Q = A_HEADS * A_HEAD_DIM
A_KV = A_KV_HEADS * A_HEAD_DIM
B_HEADS = 8
B_Q_LORA = 384
B_KV_LORA = 256
B_NOPE = 64
B_ROPE = 32
B_V = 64
B_OUT = B_HEADS * B_V
C_HEADS = 4
C_HEAD_DIM = 128
C_Q = C_HEADS * C_HEAD_DIM
N_BRANCH = 3
D_FF = 2816
IN_SIZES = (A_Q, A_KV, A_KV, B_Q_LORA, B_KV_LORA, B_ROPE, C_Q, N_BRANCH * D_MODEL)

LANES = 128
B_SLAB = LANES
B_CAT = B_HEADS * B_SLAB
OFF_CQ, OFF_CKV, OFF_QC = 0, 384, 640
W1_COLS = 1152
ROW_QA, ROW_VA, ROW_KA, ROW_KR = 0, 512, 640, 768
WT_ROWS = 800

SUB = 512
TM_FFN = 1024
FF_CHUNK = 512
TM_IN = 1024
TA = 1024
TQ = 512
TK = 2048
TKC = 256
PIPE_DEPTH = 3
V_EXT = 80
TM_MERGE = 1024
VMEM_LIMIT = 52 * 1024 * 1024
LOG2E = 1.4426950408889634

BF16 = jnp.bfloat16
F32 = jnp.float32
NT_DIMS = (((1,), (1,)), ((), ()))
TN_DIMS = (((0,), (0,)), ((), ()))


def _rms(x, g):
    return x * lax.rsqrt(jnp.mean(x * x, axis=-1, keepdims=True) + EPS) * g


def _const_spec(shape):
    nd = len(shape)
    return pl.BlockSpec(shape, lambda *_: (0,) * nd, pipeline_mode=pl.Buffered(1))


def _params(sem):
    return pltpu.CompilerParams(dimension_semantics=sem, vmem_limit_bytes=VMEM_LIMIT)


def _ones_row_tile(width):
    return (lax.broadcasted_iota(jnp.int32, (V_EXT - B_V, width), 0) == 0).astype(BF16)


def _sub_tiled(tile_fn, token_axes):
    def kernel(*refs):
        n_sub = refs[0].shape[token_axes[0][0]] * token_axes[0][1] // SUB
        for t in range(n_sub):
            views = []
            for ref, ax in zip(refs, token_axes):
                if ax is None:
                    views.append(ref)
                    continue
                axis, granule = ax
                idx = [slice(None)] * len(ref.shape)
                idx[axis] = pl.ds(t * SUB // granule, SUB // granule)
                views.append(ref.at[tuple(idx)])
            tile_fn(*views)
    return kernel


def _ffn_tile(x_ref, g_ref, w1_ref, w2_ref, fn_ref, o_ref, *, final_norm):
    x = x_ref[0]
    xn = _rms(x, g_ref[...]).astype(BF16)
    acc = jnp.zeros(x.shape, F32)
    lo = 0
    while lo < D_FF:
        fc = min(FF_CHUNK, D_FF - lo)
        g = jnp.dot(xn, w1_ref[:, lo:lo + fc], preferred_element_type=F32)
        u = jnp.dot(xn, w1_ref[:, D_FF + lo:D_FF + lo + fc], preferred_element_type=F32)
        h = (g * (1.0 / (1.0 + jnp.exp(-g))) * u).astype(BF16)
        acc = acc + jnp.dot(h, w2_ref[lo:lo + fc, :], preferred_element_type=F32)
        lo += fc
    y = x + 0.5 * acc
    if final_norm:
        y = _rms(y, fn_ref[...])
    o_ref[0] = y


def _ffn(x, g, w1, w2, fn, final_norm):
    B, S, D = x.shape
    return pl.pallas_call(
        _sub_tiled(functools.partial(_ffn_tile, final_norm=final_norm),
                   [(1, 1), None, None, None, None, (1, 1)]),
        grid=(B, S // TM_FFN),
        in_specs=[
            pl.BlockSpec((1, TM_FFN, D), lambda b, i: (b, i, 0)),
            _const_spec((1, D)),
            _const_spec((D, 2 * D_FF)),
            _const_spec((D_FF, D)),
            _const_spec((1, D)),
        ],
        out_specs=pl.BlockSpec((1, TM_FFN, D), lambda b, i: (b, i, 0)),
        out_shape=jax.ShapeDtypeStruct((B, S, D), F32),
        compiler_params=_params(("parallel", "parallel")),
        name="ffn_final" if final_norm else "ffn",
    )(x, g, w1, w2, fn)


def _rope_rows(x, tab_ref):
    half = x.shape[0] // 2
    x1, x2 = x[:half], x[half:]
    cos, sin = tab_ref[0], tab_ref[1]
    return jnp.concatenate([x1 * cos - x2 * sin, x2 * cos + x1 * sin], axis=0)


def _inproj_tile(x_ref, g_ref, w1_ref, wt_ref, ta_ref, tb_ref, qn_ref, wuqt_ref, kvn_ref,
                   wk_ref, wvt_ref, qat_ref, ka_ref, vat_ref, qt_ref, kcat_ref, vt_ref, qc_ref):
    tm = x_ref.shape[1]
    u = _rms(x_ref[0], g_ref[...]).astype(BF16)
    z = jnp.dot(u, w1_ref[...], preferred_element_type=F32)
    zt = lax.dot_general(wt_ref[...], u, NT_DIMS, preferred_element_type=F32)

    scale_a = A_HEAD_DIM ** -0.5 * LOG2E
    zeros_half = jnp.zeros((A_HEAD_DIM, tm), BF16)
    for h in range(A_HEADS):
        rows = slice(ROW_QA + h * A_HEAD_DIM, ROW_QA + (h + 1) * A_HEAD_DIM)
        q = (_rope_rows(zt[rows, :], ta_ref) * scale_a).astype(BF16)
        g_kv = h // A_GROUP
        qat_ref[0, h, g_kv * A_HEAD_DIM:(g_kv + 1) * A_HEAD_DIM, :] = q
        qat_ref[0, h, (1 - g_kv) * A_HEAD_DIM:(2 - g_kv) * A_HEAD_DIM, :] = zeros_half
    ka_t = jnp.concatenate(
        [_rope_rows(zt[ROW_KA + g * A_HEAD_DIM:ROW_KA + (g + 1) * A_HEAD_DIM, :], ta_ref)
         for g in range(A_KV_HEADS)], axis=0)
    ka_ref[0] = ka_t.T.astype(BF16)
    ones_a = _ones_row_tile(tm)
    for g_kv in range(A_KV_HEADS):
        rows = slice(ROW_VA + g_kv * A_HEAD_DIM, ROW_VA + (g_kv + 1) * A_HEAD_DIM)
        vat_ref[0, g_kv, 0:A_HEAD_DIM, :] = zt[rows, :].astype(BF16)
        vat_ref[0, g_kv, A_HEAD_DIM:V_EXT, :] = ones_a

    qc_ref[0] = z[:, OFF_QC:OFF_QC + C_Q].astype(BF16)

    cqn = _rms(z[:, OFF_CQ:OFF_CQ + B_Q_LORA], qn_ref[...]).astype(BF16)
    qt = lax.dot_general(wuqt_ref[...], cqn, NT_DIMS, preferred_element_type=F32)
    scale_b = (B_NOPE + B_ROPE) ** -0.5 * LOG2E
    rope_lo, rope_hi = B_NOPE, B_NOPE + B_ROPE
    for h in range(B_HEADS):
        slab = qt[h * B_SLAB:(h + 1) * B_SLAB, :]
        slab = jnp.concatenate([slab[:rope_lo], _rope_rows(slab[rope_lo:rope_hi], tb_ref), slab[rope_hi:]], axis=0)
        qt_ref[0, h] = (slab * scale_b).astype(BF16)

    cn = _rms(z[:, OFF_CKV:OFF_CKV + B_KV_LORA], kvn_ref[...]).astype(BF16)
    kn = jnp.dot(cn, wk_ref[...], preferred_element_type=F32)
    kr_t = jnp.concatenate([jnp.zeros((rope_lo, tm), F32),
                            _rope_rows(zt[ROW_KR:ROW_KR + B_ROPE, :], tb_ref),
                            jnp.zeros((B_SLAB - rope_hi, tm), F32)], axis=0)
    kr = kr_t.T
    for h in range(B_HEADS):
        kcat_ref[0, h] = (kn[:, h * B_SLAB:(h + 1) * B_SLAB] + kr).astype(BF16)

    vt = lax.dot_general(wvt_ref[...], cn, NT_DIMS, preferred_element_type=F32)
    ones_b = _ones_row_tile(TKC)
    for c in range(tm // TKC):
        for h in range(B_HEADS):
            vt_ref[0, c, h, 0:B_V, :] = vt[h * B_V:(h + 1) * B_V, c * TKC:(c + 1) * TKC].astype(BF16)
            vt_ref[0, c, h, B_V:V_EXT, :] = ones_b


def _inproj(x1, g, w1, wt, tab_a, tab_b, qn, wuqt, kvn, wk, wvt):
    B, S, D = x1.shape
    tok = lambda w: pl.BlockSpec((1, TM_IN, w), lambda b, i: (b, i, 0))
    tab = lambda half: pl.BlockSpec((2, half, TM_IN), lambda b, i: (0, 0, i))
    head_t = lambda heads, rows: pl.BlockSpec((1, heads, rows, TM_IN), lambda b, i: (b, 0, 0, i))
    out_shape = (
        jax.ShapeDtypeStruct((B, A_HEADS, LANES, S), BF16),
        jax.ShapeDtypeStruct((B, S, A_KV), BF16),
        jax.ShapeDtypeStruct((B, A_KV_HEADS, V_EXT, S), BF16),
        jax.ShapeDtypeStruct((B, B_HEADS, B_SLAB, S), BF16),
        jax.ShapeDtypeStruct((B, B_HEADS, S, B_SLAB), BF16),
        jax.ShapeDtypeStruct((B, S // TKC, B_HEADS, V_EXT, TKC), BF16),
        jax.ShapeDtypeStruct((B, S, C_Q), BF16),
    )
    out_specs = (
        head_t(A_HEADS, LANES), tok(A_KV), head_t(A_KV_HEADS, V_EXT), head_t(B_HEADS, B_SLAB),
        pl.BlockSpec((1, B_HEADS, TM_IN, B_SLAB), lambda b, i: (b, 0, i, 0)),
        pl.BlockSpec((1, TM_IN // TKC, B_HEADS, V_EXT, TKC), lambda b, i: (b, i, 0, 0, 0)),
        tok(C_Q),
    )
    return pl.pallas_call(
        _sub_tiled(_inproj_tile, [(1, 1), None, None, None, (2, 1), (2, 1), None, None, None, None, None,
                                  (3, 1), (1, 1), (3, 1), (3, 1), (2, 1), (1, TKC), (1, 1)]),
        grid=(B, S // TM_IN),
        in_specs=[
            tok(D), _const_spec((1, D)), _const_spec((D, W1_COLS)), _const_spec((WT_ROWS, D)),
            tab(A_HEAD_DIM // 2), tab(B_ROPE // 2),
            _const_spec((1, B_Q_LORA)), _const_spec((B_CAT, B_Q_LORA)),
            _const_spec((1, B_KV_LORA)), _const_spec((B_KV_LORA, B_CAT)),
            _const_spec((B_OUT, B_KV_LORA)),
        ],
        out_specs=out_specs,
        out_shape=out_shape,
        compiler_params=_params(("parallel", "parallel")),
        name="inproj",
    )(x1, g, w1, wt, tab_a, tab_b, qn, wuqt, kvn, wk, wvt)


def _attn_a_kernel(sink_ref, qt_ref, kp_ref, kc_ref, kn_ref, vp_ref, vc_ref, vn_ref, o_ref, *s_bufs, seq):
    i = pl.program_id(1)
    depth = len(s_bufs)
    row0 = pl.multiple_of(jnp.minimum(i, 0), 3 * BLK)
    k_all = jnp.concatenate([kp_ref[0], kc_ref[0], kn_ref[0]], axis=0)
    v_all = [jnp.concatenate([vp_ref[0, g], vc_ref[0, g], vn_ref[0, g]], axis=1)
             for g in range(A_KV_HEADS)]

    per = TA // BLK
    width = A_GROUP * BLK
    c = lax.broadcasted_iota(jnp.int32, (3 * BLK, width), 0)
    r = lax.broadcasted_iota(jnp.int32, (3 * BLK, width), 1) & (BLK - 1)
    bias_mid = jnp.where(jnp.abs(c - BLK - r) <= WINDOW, 0.0, NEG)
    bias_first = jnp.where(i * TA - BLK + c >= 0, bias_mid, NEG)
    bias_last = jnp.where(i * TA + (per - 2) * BLK + c < seq, bias_mid, NEG)
    biases = [bias_first] + [bias_mid] * (per - 2) + [bias_last]

    items = [(j, g) for j in range(per) for g in range(A_KV_HEADS)]

    def produce(n):
        j, g = items[n]
        cols = slice(j * BLK, (j + 1) * BLK)
        qt = jnp.concatenate([qt_ref[0, g * A_GROUP + hh, :, cols] for hh in range(A_GROUP)], axis=1)
        s = jnp.dot(k_all[j * BLK:(j + 3) * BLK], qt, preferred_element_type=F32)
        bias = biases[j]
        s = jnp.concatenate([s[:BLK] + bias[:BLK], s[BLK:2 * BLK], s[2 * BLK:] + bias[2 * BLK:]], axis=0)
        s_bufs[n % depth][...] = s
        return jnp.max(s, axis=0, keepdims=True)

    def consume(n, col_max):
        j, g = items[n]
        cols = slice(j * BLK, (j + 1) * BLK)
        sink = sink_ref[g]
        m = jnp.maximum(col_max, sink)
        p = jnp.exp2(s_bufs[n % depth][pl.ds(row0, 3 * BLK), :] - m).astype(BF16)
        pv = jnp.dot(v_all[g][:, j * BLK:(j + 3) * BLK], p, preferred_element_type=F32)
        denom = pv[A_HEAD_DIM:A_HEAD_DIM + 1, :] + jnp.exp2(sink - m)
        o = (pv[0:A_HEAD_DIM, :] / denom).astype(BF16)
        for hh in range(A_GROUP):
            o_ref[0, g * A_GROUP + hh, :, cols] = o[:, hh * BLK:(hh + 1) * BLK]

    col_max = {n: produce(n) for n in range(min(depth - 1, len(items)))}
    for n in range(len(items)):
        ahead = n + depth - 1
        if ahead < len(items):
            col_max[ahead] = produce(ahead)
        consume(n, col_max.pop(n))


def _attn_a(qat, ka, vat, sink_lanes):
    B, _, _, S = qat.shape
    nb = S // BLK
    per = TA // BLK
    prev_i = lambda i: jnp.maximum(i * per - 1, 0)
    next_i = lambda i: jnp.minimum((i + 1) * per, nb - 1)
    k_prev = pl.BlockSpec((1, BLK, A_KV), lambda b, i: (b, prev_i(i), 0))
    k_cur = pl.BlockSpec((1, TA, A_KV), lambda b, i: (b, i, 0))
    k_next = pl.BlockSpec((1, BLK, A_KV), lambda b, i: (b, next_i(i), 0))
    v_prev = pl.BlockSpec((1, A_KV_HEADS, V_EXT, BLK), lambda b, i: (b, 0, 0, prev_i(i)))
    v_cur = pl.BlockSpec((1, A_KV_HEADS, V_EXT, TA), lambda b, i: (b, 0, 0, i))
    v_next = pl.BlockSpec((1, A_KV_HEADS, V_EXT, BLK), lambda b, i: (b, 0, 0, next_i(i)))
    return pl.pallas_call(
        functools.partial(_attn_a_kernel, seq=S),
        grid=(B, S // TA),
        in_specs=[
            _const_spec((A_KV_HEADS, 1, A_GROUP * BLK)),
            pl.BlockSpec((1, A_HEADS, LANES, TA), lambda b, i: (b, 0, 0, i)),
            k_prev, k_cur, k_next, v_prev, v_cur, v_next,
        ],
        out_specs=pl.BlockSpec((1, A_HEADS, A_HEAD_DIM, TA), lambda b, i: (b, 0, 0, i)),
        out_shape=jax.ShapeDtypeStruct((B, A_HEADS, A_HEAD_DIM, S), BF16),
        scratch_shapes=[pltpu.VMEM((3 * BLK, A_GROUP * BLK), F32)] * PIPE_DEPTH,
        compiler_params=_params(("parallel", "parallel")),
        name="attn_a",
    )(sink_lanes, qat, ka, ka, ka, vat, vat, vat)


def _mla_kernel(qt_ref, k_ref, vt_ref, o_ref, m_ref, acc_ref, *s_bufs, n_kv):
    j = pl.program_id(2)
    items = [(c, h) for c in range(TK // TKC) for h in range(B_HEADS)]
    depth = len(s_bufs)
    row0 = pl.multiple_of(jnp.minimum(j, 0), TKC)

    @pl.when(j == 0)
    def _():
        m_ref[...] = jnp.full(m_ref.shape, NEG, F32)
        acc_ref[...] = jnp.zeros(acc_ref.shape, F32)

    def produce(n):
        c, h = items[n]
        s = jnp.dot(k_ref[0, h, c * TKC:(c + 1) * TKC, :], qt_ref[0, h],
                    preferred_element_type=F32)
        s_bufs[n % depth][...] = s
        return jnp.max(s, axis=0, keepdims=True)

    def consume(n, col_max):
        c, h = items[n]
        m_old = m_ref[h]
        m_new = jnp.maximum(m_old, col_max)
        p = jnp.exp2(s_bufs[n % depth][pl.ds(row0, TKC), :] - m_new).astype(BF16)
        pv = jnp.dot(vt_ref[0, c, h], p, preferred_element_type=F32)
        acc_ref[h] = jnp.exp2(m_old - m_new) * acc_ref[h] + pv
        m_ref[h] = m_new

    col_max = {n: produce(n) for n in range(min(depth - 1, len(items)))}
    for n in range(len(items)):
        ahead = n + depth - 1
        if ahead < len(items):
            col_max[ahead] = produce(ahead)
        consume(n, col_max.pop(n))

    @pl.when(j == n_kv - 1)
    def _():
        for h in range(B_HEADS):
            o_ref[0, h] = (acc_ref[h, 0:B_V, :] / acc_ref[h, B_V:B_V + 1, :]).astype(BF16)


def _mla(qt, kcat, vt):
    B, _, _, S = qt.shape
    return pl.pallas_call(
        functools.partial(_mla_kernel, n_kv=S // TK),
        grid=(B, S // TQ, S // TK),
        in_specs=[
            pl.BlockSpec((1, B_HEADS, B_SLAB, TQ), lambda b, i, j: (b, 0, 0, i)),
            pl.BlockSpec((1, B_HEADS, TK, B_SLAB), lambda b, i, j: (b, 0, j, 0)),
            pl.BlockSpec((1, TK // TKC, B_HEADS, V_EXT, TKC), lambda b, i, j: (b, j, 0, 0, 0)),
        ],
        out_specs=pl.BlockSpec((1, B_HEADS, B_V, TQ), lambda b, i, j: (b, 0, 0, i)),
        out_shape=jax.ShapeDtypeStruct((B, B_HEADS, B_V, S), BF16),
        scratch_shapes=[
            pltpu.VMEM((B_HEADS, 1, TQ), F32),
            pltpu.VMEM((B_HEADS, V_EXT, TQ), F32),
        ] + [pltpu.VMEM((TKC, TQ), F32)] * PIPE_DEPTH,
        compiler_params=_params(("parallel", "parallel", "arbitrary")),
        name="mla",
    )(qt, kcat, vt)


def _memkv_kernel(mem_ref, g_ref, w_ref, k_ref, v_ref):
    mn = _rms(mem_ref[0], g_ref[...]).astype(BF16)
    kv = jnp.dot(mn, w_ref[...], preferred_element_type=F32)
    k_ref[0] = kv[:, :C_Q].astype(BF16)
    v_ref[0] = kv[:, C_Q:].astype(BF16)


def _memkv(mem, g, w):
    B = mem.shape[0]
    out = jax.ShapeDtypeStruct((B, N_MEM, C_Q), BF16)
    spec = pl.BlockSpec((1, N_MEM, C_Q), lambda b: (b, 0, 0))
    return pl.pallas_call(
        _memkv_kernel,
        grid=(B,),
        in_specs=[pl.BlockSpec((1, N_MEM, D_MODEL), lambda b: (b, 0, 0)),
                  _const_spec((1, D_MODEL)), _const_spec((D_MODEL, 2 * C_Q))],
        out_specs=(spec, spec),
        out_shape=(out, out),
        compiler_params=_params(("parallel",)),
        name="memkv",
    )(mem, g, w)


def _merge_tile(x_ref, oat_ref, obt_ref, qc_ref, mk_ref, mv_ref, g_ref, wgl_ref, wa_ref, wb_ref, wc_ref,
                  wout_ref, o_ref):
    x = x_ref[0]
    tm = x.shape[0]
    u = _rms(x, g_ref[...]).astype(BF16)

    oc = []
    for h in range(C_HEADS):
        sl = slice(h * C_HEAD_DIM, (h + 1) * C_HEAD_DIM)
        s = lax.dot_general(qc_ref[0, :, sl], mk_ref[0, :, sl], NT_DIMS,
                            preferred_element_type=F32) * (C_HEAD_DIM ** -0.5)
        e = jnp.exp(s - jnp.max(s, axis=-1, keepdims=True))
        pv = jnp.dot(e.astype(BF16), mv_ref[0, :, sl], preferred_element_type=F32)
        oc.append((pv / jnp.sum(e, axis=-1, keepdims=True)).astype(BF16))
    oc = jnp.concatenate(oc, axis=1)

    branches = (
        lax.dot_general(oat_ref[0].reshape(A_Q, tm), wa_ref[...], TN_DIMS, preferred_element_type=F32),
        lax.dot_general(obt_ref[0].reshape(B_OUT, tm), wb_ref[...], TN_DIMS, preferred_element_type=F32),
        jnp.dot(oc, wc_ref[...], preferred_element_type=F32),
    )
    merged = jnp.zeros(x.shape, F32)
    for i, br in enumerate(branches):
        gl = jnp.dot(u, wgl_ref[:, i * D_MODEL:(i + 1) * D_MODEL], preferred_element_type=F32)
        merged = merged + br * (1.0 / (1.0 + jnp.exp(-gl)))
    o_ref[0] = x + jnp.dot(merged.astype(BF16), wout_ref[...], preferred_element_type=F32)


def _merge(x1, oat, obt, qc, mk, mv, g, wgl, wa, wb, wc, wout):
    B, S, D = x1.shape
    tok = lambda w: pl.BlockSpec((1, TM_MERGE, w), lambda b, i: (b, i, 0))
    head_t = lambda heads, rows: pl.BlockSpec((1, heads, rows, TM_MERGE), lambda b, i: (b, 0, 0, i))
    mem = pl.BlockSpec((1, N_MEM, C_Q), lambda b, i: (b, 0, 0))
    return pl.pallas_call(
        _sub_tiled(_merge_tile, [(1, 1), (3, 1), (3, 1), (1, 1)] + [None] * 8 + [(1, 1)]),
        grid=(B, S // TM_MERGE),
        in_specs=[
            tok(D), head_t(A_HEADS, A_HEAD_DIM), head_t(B_HEADS, B_V), tok(C_Q), mem, mem,
            _const_spec((1, D)), _const_spec((D, N_BRANCH * D)),
            _const_spec((A_Q, D)), _const_spec((B_OUT, D)), _const_spec((C_Q, D)),
            _const_spec((D, D)),
        ],
        out_specs=tok(D),
        out_shape=jax.ShapeDtypeStruct((B, S, D), F32),
        compiler_params=_params(("parallel", "parallel")),
        name="merge",
    )(x1, oat, obt, qc, mk, mv, g, wgl, wa, wb, wc, wout)


def _rope_tables(seq, half):
    inv = ROPE_THETA ** (-jnp.arange(0, 2 * half, 2, dtype=F32) / (2 * half))
    ang = inv[:, None] * jnp.arange(seq, dtype=F32)[None, :]
    return jnp.stack([jnp.cos(ang), jnp.sin(ang)])


def _prepare(ffn1_norm, ffn1_w_in, ffn1_w_out, mix_norm, w_in, mla_q_norm, mla_w_uq, mla_kv_norm,
             mla_w_ukv, attn_sink, mem_norm, w_mem_kv, w_branch_a, w_branch_b, w_branch_c, w_out,
             ffn2_norm, ffn2_w_in, ffn2_w_out, final_norm):
    row = lambda v: v.reshape(1, -1).astype(F32)
    cols, o = [], 0
    for s in IN_SIZES:
        cols.append(w_in[0][:, o:o + s])
        o += s
    w_qa, w_ka, w_va, w_cq, w_ckv, w_kr, w_qc, w_gl = cols
    w1 = jnp.concatenate([w_cq, w_ckv, w_qc], axis=1).astype(BF16)
    wt = jnp.concatenate([w_qa, w_va, w_ka, w_kr], axis=1).T.astype(BF16)
    wuq = mla_w_uq[0].reshape(B_Q_LORA, B_HEADS, B_NOPE + B_ROPE)
    wuqt = jnp.pad(wuq, ((0, 0), (0, 0), (0, B_SLAB - B_NOPE - B_ROPE))).reshape(B_Q_LORA, B_CAT).T.astype(BF16)
    wukv = mla_w_ukv[0].reshape(B_KV_LORA, B_HEADS, B_NOPE + B_V)
    wk = jnp.pad(wukv[:, :, :B_NOPE], ((0, 0), (0, 0), (0, B_SLAB - B_NOPE))).reshape(B_KV_LORA, B_CAT).astype(BF16)
    wvt = wukv[:, :, B_NOPE:].reshape(B_KV_LORA, B_OUT).T.astype(BF16)
    sink_lanes = jnp.repeat(attn_sink[0].astype(F32) * LOG2E, BLK).reshape(A_KV_HEADS, 1, A_GROUP * BLK)
    return dict(
        ffn1=(row(ffn1_norm[0]), ffn1_w_in[0].astype(BF16), ffn1_w_out[0].astype(BF16)),
        ffn2=(row(ffn2_norm[0]), ffn2_w_in[0].astype(BF16), ffn2_w_out[0].astype(BF16)),
        final=row(final_norm),
        inproj=(row(mix_norm[0]), w1, wt),
        mla=(row(mla_q_norm[0]), wuqt, row(mla_kv_norm[0]), wk, wvt),
        sink=sink_lanes,
        mem=(row(mem_norm[0]), w_mem_kv[0].astype(BF16)),
        merge=(row(mix_norm[0]), w_gl.astype(BF16), w_branch_a[0].astype(BF16), w_branch_b[0].astype(BF16),
               w_branch_c[0].astype(BF16), w_out[0].astype(BF16)),
    )


def _trunk(x, mem, w, tables):
    x1 = _ffn(x, *w["ffn1"], w["final"], final_norm=False)
    qat, ka, vat, qt, kcat, vt, qc = _inproj(x1, *w["inproj"], *tables, *w["mla"])
    oat = _attn_a(qat, ka, vat, w["sink"])
    obt = _mla(qt, kcat, vt)
    mk, mv = _memkv(mem, *w["mem"])
    x2 = _merge(x1, oat, obt, qc, mk, mv, *w["merge"])
    return _ffn(x2, *w["ffn2"], w["final"], final_norm=True)


def kernel(x_prompt, x_sample, mem_prompt, mem_sample, ffn1_norm, ffn1_w_in, ffn1_w_out, mix_norm, w_in,
           mla_q_norm, mla_w_uq, mla_kv_norm, mla_w_ukv, attn_sink, mem_norm, w_mem_kv, w_branch_a,
           w_branch_b, w_branch_c, w_out, ffn2_norm, ffn2_w_in, ffn2_w_out, final_norm):
    w = _prepare(ffn1_norm, ffn1_w_in, ffn1_w_out, mix_norm, w_in, mla_q_norm, mla_w_uq, mla_kv_norm,
                 mla_w_ukv, attn_sink, mem_norm, w_mem_kv, w_branch_a, w_branch_b, w_branch_c, w_out,
                 ffn2_norm, ffn2_w_in, ffn2_w_out, final_norm)
    seq = max(x_prompt.shape[1], x_sample.shape[1])
    tables = (_rope_tables(seq, A_HEAD_DIM // 2), _rope_tables(seq, B_ROPE // 2))
    return (_trunk(x_prompt, mem_prompt, w, tables), _trunk(x_sample, mem_sample, w, tables))
```
